```python
import math
import jax, jax.numpy as jnp
from jax import lax
import numpy as np

D_MODEL = 1024
BATCH = 16
SEQ = 4096
DEPTH = 1

N_META = 16
S5_WIDTH = D_MODEL // 2
S5_GROUP = 16
S5_GROUPS = S5_WIDTH // S5_GROUP
S5_STATE = 64
DT_MIN = 1e-3
DT_MAX = 1e-1
M_HEADS = 4
M_DK = D_MODEL // 8
M_DV = D_MODEL // 4
M_QK_WIDTH = M_HEADS * M_DK
M_V_WIDTH = M_HEADS * M_DV
M_CHUNK = 64
CONV_WIDTH = 4
D_FF = 4 * D_MODEL
ALPHA = (2.0 * DEPTH) ** 0.25
BETA = (8.0 * DEPTH) ** -0.25
LN_EPS = 1e-5
IN_SIZES = (S5_WIDTH, M_QK_WIDTH, M_QK_WIDTH, M_V_WIDTH, M_V_WIDTH, M_HEADS, M_HEADS, D_MODEL, D_MODEL)
IN_WIDTH = sum(IN_SIZES)
F_GATE_OFFSET = S5_WIDTH + 2 * M_QK_WIDTH + 2 * M_V_WIDTH + M_HEADS

kernel_name = "hybrid_s5_mlstm_gated_block"


def _layer_norm(x, g, b):
    xf = x.astype(jnp.float32)
    mu = jnp.mean(xf, axis=-1, keepdims=True)
    var = jnp.mean(jnp.square(xf - mu), axis=-1, keepdims=True)
    y = (xf - mu) * lax.rsqrt(var + LN_EPS)
    return (y * g.astype(jnp.float32) + b.astype(jnp.float32)).astype(x.dtype)


def _head_norm(h, g):
    hf = h.astype(jnp.float32)
    mu = jnp.mean(hf, axis=-1, keepdims=True)
    var = jnp.mean(jnp.square(hf - mu), axis=-1, keepdims=True)
    return (hf - mu) * lax.rsqrt(var + LN_EPS) * g.astype(jnp.float32).reshape(M_HEADS, M_DV)


def _split_columns(p):
    parts, start = [], 0
    for size in IN_SIZES:
        parts.append(p[..., start:start + size])
        start += size
    return parts


def _causal_depthwise_conv(x, w, b):
    c = x.shape[-1]
    y = lax.conv_general_dilated(
        x, w[:, None, :].astype(x.dtype), window_strides=(1,),
        padding=((CONV_WIDTH - 1, 0),), dimension_numbers=("NWC", "WIO", "NWC"),
        feature_group_count=c)
    return y + b


def _linear_recurrence_combine(left, right):
    a_l, b_l = left
    a_r, b_r = right
    return a_l * a_r, a_r * b_l + b_r


def _s5_mixer(u, lam_re, lam_im, log_dt, b_re, b_im, c_re, c_im, d_skip):
    f32 = jnp.float32
    bsz, length, _ = u.shape
    uf = u.astype(f32).reshape(bsz, length, S5_GROUPS, S5_GROUP)
    lam = lax.complex(lam_re.astype(f32), lam_im.astype(f32))
    dt = jnp.exp(log_dt.astype(f32))[:, None]
    lam_bar = jnp.exp(lam * dt)
    b_mat = lax.complex(b_re.astype(f32), b_im.astype(f32))
    b_bar = ((lam_bar - 1.0) / lam)[..., None] * b_mat
    bu = jnp.einsum("gph,blgh->blgp", b_bar, uf.astype(jnp.complex64))
    a = jnp.broadcast_to(lam_bar, (1, length, S5_GROUPS, S5_STATE))
    _, state = lax.associative_scan(_linear_recurrence_combine, (a, bu), axis=1)
    c_mat = lax.complex(c_re.astype(f32), c_im.astype(f32))
    y = jnp.real(jnp.einsum("ghp,blgp->blgh", c_mat, state))
    y = y + d_skip.astype(f32).reshape(S5_GROUPS, S5_GROUP) * uf
    return y.reshape(bsz, length, S5_WIDTH)


def _mlstm_mixer(q, k, v, i_pre, f_pre):
    f32 = jnp.float32
    bsz, length = q.shape[:2]
    n_pad = M_CHUNK - N_META
    n_chunks = (length + n_pad) // M_CHUNK

    def to_chunks(t, fill):
        t = t.astype(f32)
        t = jnp.pad(t, ((0, 0), (n_pad, 0)) + ((0, 0),) * (t.ndim - 2), constant_values=fill)
        t = t.reshape((bsz, n_chunks, M_CHUNK) + t.shape[2:])
        return jnp.moveaxis(t, (1, 3), (0, 2))

    qc = to_chunks(q, 0.0)
    kc = to_chunks(k * (M_DK ** -0.5), 0.0)
    vc = to_chunks(v, 0.0)
    log_i = to_chunks(i_pre, -jnp.inf)
    log_f = to_chunks(jax.nn.log_sigmoid(f_pre.astype(f32)), 0.0)
    causal = jnp.tril(jnp.ones((M_CHUNK, M_CHUNK), dtype=bool))

    def chunk_step(carry, inp):
        c_st, n_st, m_st = carry
        q_c, k_c, v_c, li_c, lf_c = inp
        b = jnp.cumsum(lf_c, axis=-1)
        d_mat = jnp.where(causal, b[..., :, None] - b[..., None, :] + li_c[..., None, :], -jnp.inf)
        m_inter = b + m_st[..., None]
        m_row = jnp.maximum(m_inter, jnp.max(d_mat, axis=-1))
        w_intra = jnp.exp(d_mat - m_row[..., None])
        w_inter = jnp.exp(m_inter - m_row)
        s = jnp.einsum("bhsd,bhjd->bhsj", q_c, k_c) * w_intra
        num = (jnp.einsum("bhsj,bhje->bhse", s, v_c)
               + w_inter[..., None] * jnp.einsum("bhsd,bhde->bhse", q_c, c_st))
        den = jnp.sum(s, axis=-1) + w_inter * jnp.einsum("bhsd,bhd->bhs", q_c, n_st)
        h_c = num / jnp.maximum(jnp.abs(den), jnp.exp(-m_row))[..., None]
        b_last = b[..., -1]
        g_log = b_last[..., None] - b + li_c
        m_new = jnp.maximum(b_last + m_st, jnp.max(g_log, axis=-1))
        w_k = jnp.exp(g_log - m_new[..., None])
        decay = jnp.exp(b_last + m_st - m_new)
        c_new = decay[..., None, None] * c_st + jnp.einsum("bhj,bhjd,bhje->bhde", w_k, k_c, v_c)
        n_new = decay[..., None] * n_st + jnp.einsum("bhj,bhjd->bhd", w_k, k_c)
        return (c_new, n_new, m_new), h_c

    init = (jnp.zeros((bsz, M_HEADS, M_DK, M_DV), f32),
            jnp.zeros((bsz, M_HEADS, M_DK), f32),
            jnp.zeros((bsz, M_HEADS), f32))
    _, h = lax.scan(chunk_step, init, (qc, kc, vc, log_i, log_f))
    h = jnp.moveaxis(h, (0, 2), (1, 3)).reshape(bsz, n_chunks * M_CHUNK, M_HEADS, M_DV)
    return h[:, n_pad:]


def setup_inputs(seed: int = 0) -> dict:
    key = jax.random.key(seed)
    ks = jax.random.split(key, 27)
    f32 = jnp.float32

    def nrm(k, shape, scale):
        return scale * jax.random.normal(k, shape, f32)

    b_in = nrm(ks[5], (DEPTH, IN_WIDTH), 0.02)
    b_in = b_in.at[:, F_GATE_OFFSET:F_GATE_OFFSET + M_HEADS].add(jnp.linspace(3.0, 6.0, M_HEADS))
    s5_lambda_re = -0.5 + nrm(ks[8], (DEPTH, S5_GROUPS, S5_STATE), 0.01)
    s5_lambda_im = jnp.pi * jnp.arange(S5_STATE, dtype=f32) + nrm(ks[9], (DEPTH, S5_GROUPS, S5_STATE), 0.01)
    s5_log_dt = jax.random.uniform(ks[10], (DEPTH, S5_GROUPS), f32, math.log(DT_MIN), math.log(DT_MAX))
    return {
        "x": nrm(ks[0], (BATCH, SEQ, D_MODEL), 1.0),
        "meta_tokens": nrm(ks[1], (N_META, D_MODEL), 1.0),
        "ln0_g": 1.0 + nrm(ks[2], (D_MODEL,), 0.05),
        "ln0_b": nrm(ks[3], (D_MODEL,), 0.02),
        "w_in": nrm(ks[4], (DEPTH, D_MODEL, IN_WIDTH), D_MODEL ** -0.5),
        "b_in": b_in,
        "qk_conv_w": nrm(ks[6], (DEPTH, CONV_WIDTH, 2 * M_QK_WIDTH), CONV_WIDTH ** -0.5),
        "qk_conv_b": nrm(ks[7], (DEPTH, 2 * M_QK_WIDTH), 0.02),
        "s5_lambda_re": s5_lambda_re,
        "s5_lambda_im": s5_lambda_im,
        "s5_log_dt": s5_log_dt,
        "s5_b_re": nrm(ks[11], (DEPTH, S5_GROUPS, S5_STATE, S5_GROUP), (2.0 * S5_GROUP) ** -0.5),
        "s5_b_im": nrm(ks[12], (DEPTH, S5_GROUPS, S5_STATE, S5_GROUP), (2.0 * S5_GROUP) ** -0.5),
        "s5_c_re": nrm(ks[13], (DEPTH, S5_GROUPS, S5_GROUP, S5_STATE), S5_STATE ** -0.5),
        "s5_c_im": nrm(ks[14], (DEPTH, S5_GROUPS, S5_GROUP, S5_STATE), S5_STATE ** -0.5),
        "s5_d": nrm(ks[15], (DEPTH, S5_WIDTH), 1.0),
        "s5_w_glu": nrm(ks[16], (DEPTH, S5_WIDTH, 2 * D_MODEL), S5_WIDTH ** -0.5),
        "m_norm_g": 1.0 + nrm(ks[17], (DEPTH, M_V_WIDTH), 0.05),
        "m_w_out": nrm(ks[18], (DEPTH, M_V_WIDTH, D_MODEL), M_V_WIDTH ** -0.5),
        "w_o": nrm(ks[19], (DEPTH, D_MODEL, D_MODEL), BETA * D_MODEL ** -0.5),
        "ln1_g": 1.0 + nrm(ks[20], (DEPTH, D_MODEL), 0.05),
        "ln1_b": nrm(ks[21], (DEPTH, D_MODEL), 0.02),
        "w_up": nrm(ks[22], (DEPTH, D_MODEL, D_FF), D_MODEL ** -0.5),
        "b_up": nrm(ks[23], (DEPTH, D_FF), 0.02),
        "w_down": nrm(ks[24], (DEPTH, D_FF, D_MODEL), BETA * D_FF ** -0.5),
        "ln2_g": 1.0 + nrm(ks[25], (DEPTH, D_MODEL), 0.05),
        "ln2_b": nrm(ks[26], (DEPTH, D_MODEL), 0.02),
    }


def reference(x, meta_tokens, ln0_g, ln0_b, w_in, b_in, qk_conv_w, qk_conv_b,
              s5_lambda_re, s5_lambda_im, s5_log_dt, s5_b_re, s5_b_im, s5_c_re, s5_c_im,
              s5_d, s5_w_glu, m_norm_g, m_w_out, w_o, ln1_g, ln1_b, w_up, b_up, w_down,
              ln2_g, ln2_b):
    bsz = x.shape[0]
    meta = jnp.broadcast_to(meta_tokens[None].astype(x.dtype), (bsz, N_META, D_MODEL))
    h = _layer_norm(jnp.concatenate([meta, x], axis=1), ln0_g, ln0_b)
    length = h.shape[1]
    for layer in range(DEPTH):
        p = h @ w_in[layer] + b_in[layer]
        u_s5, q, k, v, o_pre, i_pre, f_pre, g_s5, g_m = _split_columns(p)
        y_s5 = _s5_mixer(u_s5, s5_lambda_re[layer], s5_lambda_im[layer], s5_log_dt[layer],
                         s5_b_re[layer], s5_b_im[layer], s5_c_re[layer], s5_c_im[layer], s5_d[layer])
        z = jax.nn.gelu(y_s5).astype(h.dtype) @ s5_w_glu[layer]
        y_s5 = z[..., :D_MODEL] * jax.nn.sigmoid(z[..., D_MODEL:])
        qk = jax.nn.silu(_causal_depthwise_conv(jnp.concatenate([q, k], axis=-1),
                                                qk_conv_w[layer], qk_conv_b[layer]))
        q_h = qk[..., :M_QK_WIDTH].reshape(bsz, length, M_HEADS, M_DK)
        k_h = qk[..., M_QK_WIDTH:].reshape(bsz, length, M_HEADS, M_DK)
        v_h = v.reshape(bsz, length, M_HEADS, M_DV)
        hm = _mlstm_mixer(q_h, k_h, v_h, i_pre, f_pre)
        hm = _head_norm(hm, m_norm_g[layer]).reshape(bsz, length, M_V_WIDTH).astype(h.dtype)
        y_m = (jax.nn.sigmoid(o_pre) * hm) @ m_w_out[layer]
        mix = jax.nn.sigmoid(g_s5) * y_s5 + jax.nn.sigmoid(g_m) * y_m
        h = _layer_norm(ALPHA * h + mix @ w_o[layer], ln1_g[layer], ln1_b[layer])
        ff = jnp.square(jax.nn.relu(h @ w_up[layer] + b_up[layer])) @ w_down[layer]
        h = _layer_norm(ALPHA * h + ff, ln2_g[layer], ln2_b[layer])
    return h[:, N_META:]
```

```python
import functools
import math

import jax
import jax.numpy as jnp
from jax import lax
from jax.experimental import pallas as pl
from jax.experimental.pallas import tpu as pltpu

F32 = jnp.float32
BF16 = jnp.bfloat16

N_META = 16
S5_GROUP = 16
S5_STATE = 64
M_HEADS = 4
CONV_WIDTH = 4
LN_EPS = 1e-5

S5_CHUNK = 64
TILE_IN = 256
TILE_MIX = 256
TILE_FFN = 512
CARRY_ROWS = 8
V7X_VMEM_LIMIT = 56 * 1024 * 1024

NEG_INF = float("-inf")


def _layer_norm(x, g, b):
    mu = jnp.mean(x, axis=-1, keepdims=True)
    xc = x - mu
    var = jnp.mean(xc * xc, axis=-1, keepdims=True)
    return xc * lax.rsqrt(var + LN_EPS) * g + b


def _log_sigmoid(x):
    return jnp.minimum(x, 0.0) - jnp.log1p(jnp.exp(-jnp.abs(x)))


def _gelu_tanh(x):
    c = math.sqrt(2.0 / math.pi)
    return x * (0.5 * (1.0 + jnp.tanh(c * (x + 0.044715 * (x * x * x)))))


def _dot(a, b):
    return jnp.dot(a, b, preferred_element_type=F32)


def _dot_nt(a, b):
    return lax.dot_general(a, b, (((1,), (1,)), ((), ())), preferred_element_type=F32)


def _dot_tn(a, b):
    return lax.dot_general(a, b, (((0,), (0,)), ((), ())), preferred_element_type=F32)


def _split3(x):
    hi = x.astype(BF16)
    r = x - hi.astype(F32)
    mid = r.astype(BF16)
    lo = (r - mid.astype(F32)).astype(BF16)
    return hi, mid, lo


def _const_spec(shape):
    nd = len(shape)
    return pl.BlockSpec(shape, lambda *_: (0,) * nd, pipeline_mode=pl.Buffered(1))


def _inproj_kernel(x_ref, g0_ref, b0_ref, w_ref, b_ref, cw_ref, cb_ref, carry_in_ref,
                   u_ref, q_ref, k_ref, v_ref, o_ref, gs_ref, gm_ref, carry_out_ref,
                   ext_ref, *, tile, n_pad, d_s5, d_qk, d_v, d_model):
    j = pl.program_id(1)

    @pl.when(j == 0)
    def _():
        ext_ref[0:CARRY_ROWS, :] = carry_in_ref[...]

    hb = _layer_norm(x_ref[...], g0_ref[...], b0_ref[...]).astype(BF16)

    def seg(lo, width):
        return _dot(hb, w_ref[:, lo:lo + width]) + b_ref[:, lo:lo + width]

    if n_pad:
        rows = lax.broadcasted_iota(jnp.int32, (tile, 1), 0) + j * tile
        live = rows >= n_pad
    off = 0
    u = seg(off, d_s5)
    if n_pad:
        u = jnp.where(live, u, 0.0)
    u_ref[...] = u.astype(BF16)
    off += d_s5

    qk = seg(off, 2 * d_qk)
    if n_pad:
        qk = jnp.where(live, qk, 0.0)
    off += 2 * d_qk
    ext_ref[CARRY_ROWS:CARRY_ROWS + tile, :] = qk
    conv = cb_ref[...] + cw_ref[CONV_WIDTH - 1:CONV_WIDTH, :] * qk
    for r in range(CONV_WIDTH - 1):
        lag = CONV_WIDTH - 1 - r
        conv = conv + cw_ref[r:r + 1, :] * ext_ref[pl.ds(CARRY_ROWS - lag, tile), :]
    ext_ref[0:CARRY_ROWS, :] = qk[tile - CARRY_ROWS:tile, :]
    carry_out_ref[...] = qk[tile - CARRY_ROWS:tile, :]
    act = conv * jax.nn.sigmoid(conv)
    q_ref[...] = act[:, :d_qk].astype(BF16)
    k_ref[...] = (act[:, d_qk:] * ((d_qk // M_HEADS) ** -0.5)).astype(BF16)

    v_ref[...] = seg(off, d_v).astype(BF16)
    off += d_v
    o_ref[...] = jax.nn.sigmoid(seg(off, d_v)).astype(BF16)
    off += d_v
    gs_ref[...] = jax.nn.sigmoid(seg(off, d_model)).astype(BF16)
    off += d_model
    gm_ref[...] = jax.nn.sigmoid(seg(off, d_model)).astype(BF16)


def _inproj(x, g0, b0, w_main, b_main, conv_w, conv_b, carry_in, *, n_pad):
    bsz, length, d_model = x.shape
    tile = min(TILE_IN, length)
    assert length % tile == 0
    d_qk = conv_w.shape[1] // 2
    d_s5 = d_model // 2
    d_v = d_model
    n_main = w_main.shape[1]
    assert n_main == d_s5 + 2 * d_qk + 2 * d_v + 2 * d_model
    grid = (bsz, length // tile)

    def tok(width):
        return pl.BlockSpec((None, tile, width), lambda b, j: (b, j, 0))

    def out(width):
        return jax.ShapeDtypeStruct((bsz, length, width), BF16)

    kern = functools.partial(_inproj_kernel, tile=tile, n_pad=n_pad, d_s5=d_s5, d_qk=d_qk,
                             d_v=d_v, d_model=d_model)
    return pl.pallas_call(
        kern,
        grid=grid,
        in_specs=[tok(d_model), _const_spec((1, d_model)), _const_spec((1, d_model)),
                  _const_spec((d_model, n_main)), _const_spec((1, n_main)),
                  _const_spec((CONV_WIDTH, 2 * d_qk)), _const_spec((1, 2 * d_qk)),
                  _const_spec((CARRY_ROWS, 2 * d_qk))],
        out_specs=[tok(d_s5), tok(d_qk), tok(d_qk), tok(d_v), tok(d_v), tok(d_model), tok(d_model),
                   pl.BlockSpec((None, CARRY_ROWS, 2 * d_qk), lambda b, j: (b, 0, 0))],
        out_shape=[out(d_s5), out(d_qk), out(d_qk), out(d_v), out(d_v), out(d_model), out(d_model),
                   jax.ShapeDtypeStruct((bsz, CARRY_ROWS, 2 * d_qk), F32)],
        scratch_shapes=[pltpu.VMEM((CARRY_ROWS + tile, 2 * d_qk), F32)],
        compiler_params=pltpu.CompilerParams(
            dimension_semantics=("arbitrary", "arbitrary"), vmem_limit_bytes=V7X_VMEM_LIMIT),
        name="inproj",
    )(x, g0, b0, w_main, b_main, conv_w, conv_b, carry_in)


def _s5_kernel(u_ref, toe_ref, wst_ref, wc_ref, a1_ref, a2_ref, dsk_ref, y_ref, s_ref, xin_ref,
               *, n_chunks, bsz):
    u = u_ref[...]
    half = 2 * S5_STATE
    s_ref[...] = _dot(u, wst_ref[...])
    a1 = a1_ref[...]
    a2 = a2_ref[...]

    def step(c, state):
        r = pl.multiple_of(c * bsz, bsz)
        xin_ref[pl.ds(r, bsz), :] = state[:, :half]
        swapped = jnp.concatenate([state[:, half:], state[:, :half]], axis=1)
        return a1 * state + a2 * swapped + s_ref[pl.ds(r, bsz), :]

    lax.fori_loop(0, n_chunks, step, jnp.zeros((bsz, 2 * half), F32))
    y = _dot(u, toe_ref[...]) + _dot(xin_ref[...].astype(BF16), wc_ref[...])
    y_ref[...] = (y + u.astype(F32) * dsk_ref[...]).astype(y_ref.dtype)


def _s5(u_rows, toe, wst, wc, a1, a2, dskip, *, n_chunks, bsz):
    groups, rows, width = u_rows.shape
    half = 2 * S5_STATE

    def grp(*shape):
        return pl.BlockSpec((None,) + shape, lambda g: (g,) + (0,) * len(shape))

    return pl.pallas_call(
        functools.partial(_s5_kernel, n_chunks=n_chunks, bsz=bsz),
        grid=(groups,),
        in_specs=[grp(rows, width), grp(width, width), grp(width, 2 * half), grp(half, width),
                  grp(1, 2 * half), grp(1, 2 * half), grp(1, width)],
        out_specs=grp(rows, width),
        out_shape=jax.ShapeDtypeStruct((groups, rows, width), BF16),
        scratch_shapes=[pltpu.VMEM((rows, 2 * half), F32), pltpu.VMEM((rows, half), F32)],
        compiler_params=pltpu.CompilerParams(
            dimension_semantics=("arbitrary",), vmem_limit_bytes=V7X_VMEM_LIMIT),
        name="s5",
    )(u_rows, toe, wst, wc, a1, a2, dskip)


def _s5_weights(lam_re, lam_im, log_dt, b_re, b_im, c_re, c_im, d_skip):
    hp = lax.Precision.HIGHEST
    t_len = S5_CHUNK
    groups, n_state = lam_re.shape
    lam = lax.complex(lam_re.astype(F32), lam_im.astype(F32))
    dt = jnp.exp(log_dt.astype(F32))[:, None]
    lam_dt = lam * dt
    lam_bar = jnp.exp(lam_dt)
    b_bar = ((lam_bar - 1.0) / lam)[..., None] * lax.complex(b_re.astype(F32), b_im.astype(F32))
    c_mat = lax.complex(c_re.astype(F32), c_im.astype(F32))
    steps = jnp.arange(t_len + 1, dtype=F32)
    pw = jnp.exp(lam_dt[None] * steps[:, None, None].astype(jnp.complex64))

    cp = c_mat[None] * pw[:, :, None, :]
    kern = jnp.real(jnp.einsum("dgip,gpj->dgij", cp[:t_len], b_bar, precision=hp))
    s_idx = jnp.arange(t_len)[:, None]
    t_idx = jnp.arange(t_len)[None, :]
    lag = t_idx - s_idx
    blocks = jnp.where((lag >= 0)[:, :, None, None, None], kern[jnp.maximum(lag, 0)], 0.0)
    toe = jnp.transpose(blocks, (2, 0, 4, 1, 3)).reshape(
        groups, t_len * S5_GROUP, t_len * S5_GROUP)

    pb = pw[t_len - 1 - jnp.arange(t_len)][..., None] * b_bar[None]
    pb = jnp.transpose(pb, (1, 0, 3, 2))
    pre, pim = jnp.real(pb), jnp.imag(pb)
    wst = jnp.concatenate([pre, pim, pim, pre], axis=-1).reshape(
        groups, t_len * S5_GROUP, 4 * n_state)

    cq = jnp.transpose(cp[1:], (1, 3, 0, 2))
    wc = jnp.concatenate([jnp.real(cq), -jnp.imag(cq)], axis=1).reshape(
        groups, 2 * n_state, t_len * S5_GROUP)

    a_re, a_im = jnp.real(pw[t_len]), jnp.imag(pw[t_len])
    a1 = jnp.concatenate([a_re, a_re, a_re, a_re], axis=-1)[:, None, :]
    a2 = jnp.concatenate([-a_im, a_im, a_im, -a_im], axis=-1)[:, None, :]
    dsk = jnp.tile(d_skip.astype(F32).reshape(groups, 1, S5_GROUP), (1, 1, t_len))
    return toe.astype(BF16), wst.astype(BF16), wc.astype(BF16), a1, a2, dsk


def _mixer_kernel(x_ref, ys_ref, q_ref, k_ref, v_ref, o_ref, gs_ref, gm_ref,
                  g0_ref, b0_ref, wgc_ref, bgc_ref, wgr_ref, bgr_ref,
                  wglu_ref, mng_ref, wmo_ref, wo_ref, g1_ref, b1_ref,
                  c0_ref, n0_ref, m0_ref,
                  *refs, tile, n_pad, alpha, emit_state, d_model):
    if emit_state:
        out_ref, c_out_ref, n_out_ref, m_out_ref, c_ref, n_ref, m_ref, hm_ref = refs
    else:
        out_ref, c_ref, n_ref, m_ref, hm_ref = refs
    j = pl.program_id(1)
    dk = q_ref.shape[-1] // M_HEADS
    dv = v_ref.shape[-1] // M_HEADS

    @pl.when(j == 0)
    def _():
        c_ref[...] = c0_ref[...]
        n_ref[...] = n0_ref[...]
        m_ref[...] = m0_ref[...]

    h0 = _layer_norm(x_ref[...], g0_ref[...], b0_ref[...])
    hb = h0.astype(BF16)

    g_col = _dot(hb, wgc_ref[...]) + bgc_ref[...]
    g_row = _dot_nt(wgr_ref[...], hb) + bgr_ref[...]
    lane = lax.broadcasted_iota(jnp.int32, g_col.shape, 1)
    sub = lax.broadcasted_iota(jnp.int32, g_row.shape, 0)
    is_f_col = (lane >= M_HEADS) & (lane < 2 * M_HEADS)
    is_f_row = sub >= M_HEADS
    lf_col = jnp.where(is_f_col, _log_sigmoid(g_col), 0.0)
    lf_row = jnp.where(is_f_row, _log_sigmoid(g_row), 0.0)
    li_col = g_col
    li_row = g_row
    if n_pad:
        t_col = lax.broadcasted_iota(jnp.int32, g_col.shape, 0) + j * tile
        t_row = lax.broadcasted_iota(jnp.int32, g_row.shape, 1) + j * tile
        lf_col = jnp.where(t_col >= n_pad, lf_col, 0.0)
        lf_row = jnp.where(t_row >= n_pad, lf_row, 0.0)
        li_col = jnp.where(t_col >= n_pad, li_col, NEG_INF)
        li_row = jnp.where(t_row >= n_pad, li_row, NEG_INF)

    ti = lax.broadcasted_iota(jnp.int32, (tile, tile), 0)
    tj = lax.broadcasted_iota(jnp.int32, (tile, tile), 1)
    causal = ti >= tj
    tri_lower = jnp.where(causal, 1.0, 0.0).astype(BF16)
    tri_upper = jnp.where(ti <= tj, 1.0, 0.0).astype(BF16)
    b_col = sum(_dot(tri_lower, part) for part in _split3(lf_col))
    b_row = sum(_dot(part, tri_upper) for part in _split3(lf_row))

    for h in range(M_HEADS):
        qh = q_ref[:, h * dk:(h + 1) * dk]
        kh = k_ref[:, h * dk:(h + 1) * dk]
        vh = v_ref[:, h * dv:(h + 1) * dv]
        bc = b_col[:, M_HEADS + h:M_HEADS + h + 1]
        lic = li_col[:, h:h + 1]
        br = b_row[M_HEADS + h:M_HEADS + h + 1, :]
        lir = li_row[h:h + 1, :]
        m_st = m_ref[h:h + 1, 0:1]
        c_st = c_ref[h]
        n_st = n_ref[h:h + 1, :]

        d_mat = jnp.where(causal, bc - br + lir, NEG_INF)
        m_inter = bc + m_st
        m_row = jnp.maximum(m_inter, jnp.max(d_mat, axis=-1, keepdims=True))
        w_intra = jnp.exp(d_mat - m_row)
        w_inter = jnp.exp(m_inter - m_row)
        s_mat = _dot_nt(qh, kh) * w_intra
        num = _dot(s_mat.astype(BF16), vh) + w_inter * _dot(qh, c_st.astype(BF16))
        den = (jnp.sum(s_mat, axis=-1, keepdims=True)
               + w_inter * jnp.sum(qh.astype(F32) * n_st, axis=-1, keepdims=True))
        hh = num / jnp.maximum(jnp.abs(den), jnp.exp(-m_row))

        b_last = br[:, tile - 1:tile]
        g_col_log = b_last - bc + lic
        g_row_log = b_last - br + lir
        m_new = jnp.maximum(b_last + m_st, jnp.max(g_row_log, axis=-1, keepdims=True))
        w_k = jnp.exp(g_col_log - m_new)
        decay = jnp.exp(b_last + m_st - m_new)
        kw = kh.astype(F32) * w_k
        c_ref[h] = decay * c_st + _dot_tn(kw.astype(BF16), vh)
        n_ref[h:h + 1, :] = decay * n_st + jnp.sum(kw, axis=0, keepdims=True)
        m_ref[h:h + 1, :] = jnp.broadcast_to(m_new, (1, m_ref.shape[1]))

        mu = jnp.mean(hh, axis=-1, keepdims=True)
        hc = hh - mu
        var = jnp.mean(hc * hc, axis=-1, keepdims=True)
        hn = hc * lax.rsqrt(var + LN_EPS) * mng_ref[:, h * dv:(h + 1) * dv]
        hm_ref[:, h * dv:(h + 1) * dv] = (
            o_ref[:, h * dv:(h + 1) * dv].astype(F32) * hn).astype(BF16)

    y_m = _dot(hm_ref[...], wmo_ref[...])
    z = _dot(_gelu_tanh(ys_ref[...].astype(F32)).astype(BF16), wglu_ref[...])
    y_s5 = z[:, :d_model] * jax.nn.sigmoid(z[:, d_model:])
    mix = gs_ref[...].astype(F32) * y_s5 + gm_ref[...].astype(F32) * y_m
    res = alpha * h0 + _dot(mix.astype(BF16), wo_ref[...])
    out_ref[...] = _layer_norm(res, g1_ref[...], b1_ref[...])
    if emit_state:
        c_out_ref[...] = c_ref[...]
        n_out_ref[...] = n_ref[...]
        m_out_ref[...] = m_ref[...]


def _mixer(x, ys, q, k, v, o, gs, gm, g0, b0, wgc, bgc, wgr, bgr, wglu, mng, wmo, wo, g1, b1,
           c0, n0, m0, *, n_pad, alpha, emit_state):
    bsz, length, d_model = x.shape
    tile = TILE_MIX
    assert length % tile == 0
    d_qk = q.shape[-1]
    d_v = v.shape[-1]
    d_s5 = ys.shape[-1]
    dk, dv = d_qk // M_HEADS, d_v // M_HEADS
    grid = (bsz, length // tile)

    def tok(width):
        return pl.BlockSpec((None, tile, width), lambda b, j: (b, j, 0))

    consts = [g0, b0, wgc, bgc, wgr, bgr, wglu, mng, wmo, wo, g1, b1, c0, n0, m0]
    in_specs = ([tok(d_model), tok(d_s5), tok(d_qk), tok(d_qk), tok(d_v), tok(d_v), tok(d_model),
                 tok(d_model)] + [_const_spec(c.shape) for c in consts])
    out_specs = [tok(d_model)]
    out_shape = [jax.ShapeDtypeStruct((bsz, length, d_model), F32)]
    if emit_state:
        out_specs += [pl.BlockSpec((None, M_HEADS, dk, dv), lambda b, j: (b, 0, 0, 0)),
                      pl.BlockSpec((None, 8, dk), lambda b, j: (b, 0, 0)),
                      pl.BlockSpec((None, 8, dk), lambda b, j: (b, 0, 0))]
        out_shape += [jax.ShapeDtypeStruct((bsz, M_HEADS, dk, dv), F32),
                      jax.ShapeDtypeStruct((bsz, 8, dk), F32),
                      jax.ShapeDtypeStruct((bsz, 8, dk), F32)]
    kern = functools.partial(_mixer_kernel, tile=tile, n_pad=n_pad, alpha=alpha,
                             emit_state=emit_state, d_model=d_model)
    return pl.pallas_call(
        kern,
        grid=grid,
        in_specs=in_specs,
        out_specs=out_specs,
        out_shape=out_shape,
        scratch_shapes=[pltpu.VMEM((M_HEADS, dk, dv), F32), pltpu.VMEM((8, dk), F32),
                        pltpu.VMEM((8, dk), F32), pltpu.VMEM((tile, d_v), BF16)],
        compiler_params=pltpu.CompilerParams(
            dimension_semantics=("arbitrary", "arbitrary"), vmem_limit_bytes=V7X_VMEM_LIMIT),
        name="mixer",
    )(x, ys, q, k, v, o, gs, gm, *consts)


def _ffn_kernel(h_ref, wu_ref, bu_ref, wd_ref, g2_ref, b2_ref, out_ref, *, alpha, ff_chunk):
    h = h_ref[...]
    hb = h.astype(BF16)
    d_ff = wu_ref.shape[1]
    acc = alpha * h
    for c in range(d_ff // ff_chunk):
        lo = c * ff_chunk
        a = jnp.maximum(_dot(hb, wu_ref[:, lo:lo + ff_chunk]) + bu_ref[:, lo:lo + ff_chunk], 0.0)
        acc = acc + _dot((a * a).astype(BF16), wd_ref[lo:lo + ff_chunk, :])
    out_ref[...] = _layer_norm(acc, g2_ref[...], b2_ref[...])


def _ffn(h, wu, bu, wd, g2, b2, *, alpha):
    rows, d_model = h.shape
    tile = TILE_FFN
    assert rows % tile == 0
    d_ff = wu.shape[1]
    return pl.pallas_call(
        functools.partial(_ffn_kernel, alpha=alpha, ff_chunk=min(1024, d_ff)),
        grid=(rows // tile,),
        in_specs=[pl.BlockSpec((tile, d_model), lambda i: (i, 0)),
                  _const_spec((d_model, d_ff)), _const_spec((1, d_ff)),
                  _const_spec((d_ff, d_model)), _const_spec((1, d_model)),
                  _const_spec((1, d_model))],
        out_specs=pl.BlockSpec((tile, d_model), lambda i: (i, 0)),
        out_shape=jax.ShapeDtypeStruct((rows, d_model), F32),
        compiler_params=pltpu.CompilerParams(
            dimension_semantics=("arbitrary",), vmem_limit_bytes=V7X_VMEM_LIMIT),
        name="ffn",
    )(h, wu, bu, wd, g2, b2)


def kernel(x, meta_tokens, ln0_g, ln0_b, w_in, b_in, qk_conv_w, qk_conv_b, s5_lambda_re, s5_lambda_im, s5_log_dt, s5_b_re, s5_b_im, s5_c_re, s5_c_im, s5_d, s5_w_glu, m_norm_g, m_w_out, w_o, ln1_g, ln1_b, w_up, b_up, w_down, ln2_g, ln2_b):
    bsz, seq, d_model = x.shape
    depth = w_in.shape[0]
    assert depth == 1, "the meta-token prologue is written for a single layer"
    assert meta_tokens.shape == (N_META, d_model)
    assert seq % TILE_MIX == 0 and seq % S5_CHUNK == 0 and TILE_MIX >= S5_CHUNK >= N_META
    alpha = (2.0 * depth) ** 0.25
    d_s5 = d_model // 2
    d_qk = qk_conv_w.shape[-1] // 2
    d_v = m_norm_g.shape[-1]
    groups = d_s5 // S5_GROUP
    dk, dv = d_qk // M_HEADS, d_v // M_HEADS
    gate_lo = d_s5 + 2 * d_qk + 2 * d_v
    gate_hi = gate_lo + 2 * M_HEADS
    assert w_in.shape[-1] == gate_hi + 2 * d_model

    row = lambda a: a.reshape(1, -1).astype(F32)
    w = w_in[0]
    bias = b_in[0]
    w_main = jnp.concatenate([w[:, :gate_lo], w[:, gate_hi:]], axis=1).astype(BF16)
    b_main = row(jnp.concatenate([bias[:gate_lo], bias[gate_hi:]]))
    w_gate = w[:, gate_lo:gate_hi]
    wgc = jnp.pad(w_gate, ((0, 0), (0, 128 - 2 * M_HEADS))).astype(BF16)
    bgc = jnp.pad(bias[gate_lo:gate_hi], (0, 128 - 2 * M_HEADS)).reshape(1, 128).astype(F32)
    wgr = w_gate.T.astype(BF16)
    bgr = bias[gate_lo:gate_hi].reshape(2 * M_HEADS, 1).astype(F32)
    g0, b0 = row(ln0_g), row(ln0_b)
    conv_w = qk_conv_w[0].astype(F32)
    conv_b = row(qk_conv_b[0])
    mixer_consts = (g0, b0, wgc, bgc, wgr, bgr, s5_w_glu[0].astype(BF16), row(m_norm_g[0]),
                    m_w_out[0].astype(BF16), w_o[0].astype(BF16), row(ln1_g[0]), row(ln1_b[0]))
    inproj_consts = (g0, b0, w_main, b_main, conv_w, conv_b)

    pad = TILE_MIX - N_META
    x_meta = jnp.concatenate([jnp.zeros((pad, d_model), x.dtype), meta_tokens.astype(x.dtype)])[None]
    u_m, q_m, k_m, v_m, o_m, gs_m, gm_m, carry_m = _inproj(
        x_meta, *inproj_consts, jnp.zeros((CARRY_ROWS, 2 * d_qk), F32), n_pad=pad)
    zero_state = (jnp.zeros((M_HEADS, dk, dv), F32), jnp.zeros((8, dk), F32),
                  jnp.zeros((8, dk), F32))
    _, c_m, n_m, m_m = _mixer(
        x_meta, jnp.zeros((1, TILE_MIX, d_s5), BF16), q_m, k_m, v_m, o_m, gs_m, gm_m,
        *mixer_consts, *zero_state, n_pad=pad, alpha=alpha, emit_state=True)

    u, q, k, v, o, gs, gm, _ = _inproj(x, *inproj_consts, carry_m[0], n_pad=0)

    n_chunks = seq // S5_CHUNK
    width = S5_CHUNK * S5_GROUP
    u_rows = u.reshape(bsz, n_chunks, S5_CHUNK, groups, S5_GROUP)
    u_rows = jnp.transpose(u_rows, (3, 1, 0, 2, 4)).reshape(groups, n_chunks, bsz, width)
    u_meta = u_m[0, TILE_MIX - S5_CHUNK:].reshape(S5_CHUNK, groups, S5_GROUP)
    u_meta = jnp.transpose(u_meta, (1, 0, 2)).reshape(groups, 1, 1, width)
    u_rows = jnp.concatenate([jnp.broadcast_to(u_meta, (groups, 1, bsz, width)), u_rows], axis=1)
    u_rows = u_rows.reshape(groups, (n_chunks + 1) * bsz, width)
    s5_consts = _s5_weights(s5_lambda_re[0], s5_lambda_im[0], s5_log_dt[0], s5_b_re[0], s5_b_im[0],
                            s5_c_re[0], s5_c_im[0], s5_d[0])
    y_rows = _s5(u_rows, *s5_consts, n_chunks=n_chunks + 1, bsz=bsz)
    ys = y_rows[:, bsz:].reshape(groups, n_chunks, bsz, S5_CHUNK, S5_GROUP)
    ys = jnp.transpose(ys, (2, 1, 3, 0, 4)).reshape(bsz, seq, d_s5)

    (h1,) = _mixer(x, ys, q, k, v, o, gs, gm, *mixer_consts, c_m[0], n_m[0], m_m[0],
                   n_pad=0, alpha=alpha, emit_state=False)
    out = _ffn(h1.reshape(bsz * seq, d_model), w_up[0].astype(BF16), row(b_up[0]),
               w_down[0].astype(BF16), row(ln2_g[0]), row(ln2_b[0]), alpha=alpha)
    return out.reshape(bsz, seq, d_model)
```

```python
import functools
import math

import jax
import jax.numpy as jnp
from jax import lax
from jax.experimental import pallas as pl
from jax.experimental.pallas import tpu as pltpu

F32 = jnp.float32
BF16 = jnp.bfloat16

N_META = 16
S5_GROUP = 16
S5_STATE = 64
M_HEADS = 4
CONV_WIDTH = 4
LN_EPS = 1e-5

LANES = 128
S5_CHUNK = LANES
TILE_IN = 256
TILE_MIX = 256
TILE_FFN = 512
CARRY_ROWS = 8
V7X_VMEM_LIMIT = 56 * 1024 * 1024

NEG_INF = float("-inf")


def _layer_norm(x, g, b):
    mu = jnp.mean(x, axis=-1, keepdims=True)
    xc = x - mu
    var = jnp.mean(xc * xc, axis=-1, keepdims=True)
    return xc * lax.rsqrt(var + LN_EPS) * g + b


def _log_sigmoid(x):
    return jnp.minimum(x, 0.0) - jnp.log1p(jnp.exp(-jnp.abs(x)))


def _gelu_tanh(x):
    c = math.sqrt(2.0 / math.pi)
    return x * (0.5 * (1.0 + jnp.tanh(c * (x + 0.044715 * (x * x * x)))))


def _dot(a, b):
    return jnp.dot(a, b, preferred_element_type=F32)


def _dot_nt(a, b):
    return lax.dot_general(a, b, (((1,), (1,)), ((), ())), preferred_element_type=F32)


def _dot_tn(a, b):
    return lax.dot_general(a, b, (((0,), (0,)), ((), ())), preferred_element_type=F32)


def _split3(x):
    hi = x.astype(BF16)
    r = x - hi.astype(F32)
    mid = r.astype(BF16)
    lo = (r - mid.astype(F32)).astype(BF16)
    return hi, mid, lo


def _const_spec(shape):
    nd = len(shape)
    return pl.BlockSpec(shape, lambda *_: (0,) * nd, pipeline_mode=pl.Buffered(1))


def _inproj_kernel(x_ref, g0_ref, b0_ref, wut_ref, but_ref, w_ref, b_ref, cw_ref, cb_ref,
                   carry_in_ref,
                   ut_ref, q_ref, k_ref, v_ref, o_ref, gs_ref, gm_ref, carry_out_ref,
                   ext_ref, *, tile, n_pad, d_qk, d_v, d_model):
    j = pl.program_id(1)

    @pl.when(j == 0)
    def _():
        ext_ref[0:CARRY_ROWS, :] = carry_in_ref[...]

    hb = _layer_norm(x_ref[...], g0_ref[...], b0_ref[...]).astype(BF16)

    def seg(lo, width):
        return _dot(hb, w_ref[:, lo:lo + width]) + b_ref[:, lo:lo + width]

    ut = _dot_nt(wut_ref[...], hb) + but_ref[...]
    if n_pad:
        cols = lax.broadcasted_iota(jnp.int32, (1, tile), 1) + j * tile
        ut = jnp.where(cols >= n_pad, ut, 0.0)
    ut_ref[...] = ut.astype(BF16)

    off = 0
    qk = seg(off, 2 * d_qk)
    if n_pad:
        rows = lax.broadcasted_iota(jnp.int32, (tile, 1), 0) + j * tile
        qk = jnp.where(rows >= n_pad, qk, 0.0)
    off += 2 * d_qk
    ext_ref[CARRY_ROWS:CARRY_ROWS + tile, :] = qk
    conv = cb_ref[...] + cw_ref[CONV_WIDTH - 1:CONV_WIDTH, :] * qk
    for r in range(CONV_WIDTH - 1):
        lag = CONV_WIDTH - 1 - r
        conv = conv + cw_ref[r:r + 1, :] * ext_ref[pl.ds(CARRY_ROWS - lag, tile), :]
    ext_ref[0:CARRY_ROWS, :] = qk[tile - CARRY_ROWS:tile, :]
    carry_out_ref[...] = qk[tile - CARRY_ROWS:tile, :]
    act = conv * jax.nn.sigmoid(conv)
    q_ref[...] = act[:, :d_qk].astype(BF16)
    k_ref[...] = (act[:, d_qk:] * ((d_qk // M_HEADS) ** -0.5)).astype(BF16)

    v_ref[...] = seg(off, d_v).astype(BF16)
    off += d_v
    o_ref[...] = jax.nn.sigmoid(seg(off, d_v)).astype(BF16)
    off += d_v
    gs_ref[...] = jax.nn.sigmoid(seg(off, d_model)).astype(BF16)
    off += d_model
    gm_ref[...] = jax.nn.sigmoid(seg(off, d_model)).astype(BF16)


def _inproj(x, g0, b0, w_ut, b_ut, w_main, b_main, conv_w, conv_b, carry_in, *, n_pad):
    bsz, length, d_model = x.shape
    tile = min(TILE_IN, length)
    assert length % tile == 0
    d_qk = conv_w.shape[1] // 2
    d_s5 = w_ut.shape[0]
    d_v = d_model
    n_main = w_main.shape[1]
    assert n_main == 2 * d_qk + 2 * d_v + 2 * d_model
    grid = (bsz, length // tile)

    def tok(width):
        return pl.BlockSpec((None, tile, width), lambda b, j: (b, j, 0))

    def out(width):
        return jax.ShapeDtypeStruct((bsz, length, width), BF16)

    kern = functools.partial(_inproj_kernel, tile=tile, n_pad=n_pad, d_qk=d_qk, d_v=d_v,
                             d_model=d_model)
    return pl.pallas_call(
        kern,
        grid=grid,
        in_specs=[tok(d_model), _const_spec((1, d_model)), _const_spec((1, d_model)),
                  _const_spec((d_s5, d_model)), _const_spec((d_s5, 1)),
                  _const_spec((d_model, n_main)), _const_spec((1, n_main)),
                  _const_spec((CONV_WIDTH, 2 * d_qk)), _const_spec((1, 2 * d_qk)),
                  _const_spec((CARRY_ROWS, 2 * d_qk))],
        out_specs=[pl.BlockSpec((None, d_s5, tile), lambda b, j: (b, 0, j)),
                   tok(d_qk), tok(d_qk), tok(d_v), tok(d_v), tok(d_model), tok(d_model),
                   pl.BlockSpec((None, CARRY_ROWS, 2 * d_qk), lambda b, j: (b, 0, 0))],
        out_shape=[jax.ShapeDtypeStruct((bsz, d_s5, length), BF16),
                   out(d_qk), out(d_qk), out(d_v), out(d_v), out(d_model), out(d_model),
                   jax.ShapeDtypeStruct((bsz, CARRY_ROWS, 2 * d_qk), F32)],
        scratch_shapes=[pltpu.VMEM((CARRY_ROWS + tile, 2 * d_qk), F32)],
        compiler_params=pltpu.CompilerParams(
            dimension_semantics=("arbitrary", "arbitrary"), vmem_limit_bytes=V7X_VMEM_LIMIT),
        name="inproj",
    )(x, g0, b0, w_ut, b_ut, w_main, b_main, conv_w, conv_b, carry_in)


def _s5_kernel(ut_ref, um_ref, taps_ref, wst_ref, wc_ref, a1_ref, a2_ref, dsk_ref, yt_ref,
               px_ref, lhs_ref, toe_ref, s_ref, xin_ref, y_ref, ymid_ref, *, n_chunks, bsz):
    t_len = S5_CHUNK
    n_ch = S5_GROUP
    length = n_chunks * t_len
    half = 2 * S5_STATE
    rows_bj = bsz * n_ch
    lane_step = 1024

    r_idx = lax.broadcasted_iota(jnp.int32, (rows_bj, rows_bj), 0)
    q_idx = lax.broadcasted_iota(jnp.int32, (rows_bj, rows_bj), 1)
    perm = jnp.where(q_idx == (r_idx % bsz) * n_ch + r_idx // bsz, 1.0, 0.0).astype(BF16)
    perm_back = jnp.where(q_idx == (r_idx % n_ch) * bsz + r_idx // n_ch, 1.0, 0.0).astype(BF16)
    x_bj = ut_ref[...].reshape(rows_bj, length)
    for lo in range(0, length, lane_step):
        px_ref[:, lo:lo + lane_step] = _dot(perm, x_bj[:, lo:lo + lane_step]).astype(BF16)

    for jj in range(n_ch):
        lhs_ref[0:bsz, jj * t_len:(jj + 1) * t_len] = jnp.broadcast_to(
            um_ref[jj:jj + 1, :], (bsz, t_len))
    for c in range(n_chunks):
        for jj in range(n_ch):
            lhs_ref[(c + 1) * bsz:(c + 2) * bsz, jj * t_len:(jj + 1) * t_len] = (
                px_ref[jj * bsz:(jj + 1) * bsz, c * t_len:(c + 1) * t_len])

    s_io = lax.broadcasted_iota(jnp.int32, (t_len, t_len), 0)
    t_io = lax.broadcasted_iota(jnp.int32, (t_len, t_len), 1)
    lower = t_io >= s_io

    def toe_rows(jj, carry):
        r0 = pl.multiple_of(jj * t_len, t_len)
        for ii in range(n_ch):
            tap = jnp.broadcast_to(taps_ref[jj, ii:ii + 1, :], (t_len, t_len))
            blk = pltpu.roll(tap, 0, 1, stride=1, stride_axis=0)
            toe_ref[pl.ds(r0, t_len), ii * t_len:(ii + 1) * t_len] = (
                jnp.where(lower, blk, 0.0).astype(BF16))
        return carry

    lax.fori_loop(0, n_ch, toe_rows, 0)

    lhs = lhs_ref[...]
    s_ref[...] = _dot(lhs, wst_ref[...])
    a1 = a1_ref[...]
    a2 = a2_ref[...]

    def step(c, state):
        r = pl.multiple_of(c * bsz, bsz)
        xin_ref[pl.ds(r, bsz), :] = state[:, :half]
        swapped = jnp.concatenate([state[:, half:], state[:, :half]], axis=1)
        return a1 * state + a2 * swapped + s_ref[pl.ds(r, bsz), :]

    lax.fori_loop(0, n_chunks + 1, step, jnp.zeros((bsz, 2 * half), F32))
    y_ref[...] = (_dot(lhs, toe_ref[...]) + _dot(xin_ref[...].astype(BF16), wc_ref[...])
                  + lhs.astype(F32) * dsk_ref[...])

    for c in range(n_chunks):
        for ii in range(n_ch):
            ymid_ref[ii * bsz:(ii + 1) * bsz, c * t_len:(c + 1) * t_len] = (
                y_ref[(c + 1) * bsz:(c + 2) * bsz, ii * t_len:(ii + 1) * t_len].astype(BF16))
    for lo in range(0, length, lane_step):
        yt_ref[:, :, lo:lo + lane_step] = _dot(
            perm_back, ymid_ref[:, lo:lo + lane_step]).astype(BF16).reshape(bsz, n_ch, lane_step)


def _s5(ut, um, taps, wst, wc, a1, a2, dskip):
    bsz, d_s5, length = ut.shape
    groups = d_s5 // S5_GROUP
    t_len = S5_CHUNK
    n_chunks = length // t_len
    assert length % 1024 == 0
    width = S5_GROUP * t_len
    half = 2 * S5_STATE
    rows = (n_chunks + 1) * bsz

    def grp(*shape):
        return pl.BlockSpec((None,) + shape, lambda g: (g,) + (0,) * len(shape))

    return pl.pallas_call(
        functools.partial(_s5_kernel, n_chunks=n_chunks, bsz=bsz),
        grid=(groups,),
        in_specs=[pl.BlockSpec((bsz, S5_GROUP, length), lambda g: (0, g, 0)),
                  pl.BlockSpec((S5_GROUP, t_len), lambda g: (g, 0)),
                  grp(S5_GROUP, S5_GROUP, t_len), grp(width, 2 * half), grp(half, width),
                  grp(1, 2 * half), grp(1, 2 * half), grp(1, width)],
        out_specs=pl.BlockSpec((bsz, S5_GROUP, length), lambda g: (0, g, 0)),
        out_shape=jax.ShapeDtypeStruct((bsz, d_s5, length), BF16),
        scratch_shapes=[pltpu.VMEM((bsz * S5_GROUP, length), BF16),
                        pltpu.VMEM((rows, width), BF16),
                        pltpu.VMEM((width, width), BF16),
                        pltpu.VMEM((rows, 2 * half), F32),
                        pltpu.VMEM((rows, half), F32),
                        pltpu.VMEM((rows, width), F32),
                        pltpu.VMEM((bsz * S5_GROUP, length), BF16)],
        compiler_params=pltpu.CompilerParams(
            dimension_semantics=("arbitrary",), vmem_limit_bytes=V7X_VMEM_LIMIT),
        name="s5",
    )(ut, um, taps, wst, wc, a1, a2, dskip)


def _s5_weights(lam_re, lam_im, log_dt, b_re, b_im, c_re, c_im, d_skip):
    hp = lax.Precision.HIGHEST
    t_len = S5_CHUNK
    groups, n_state = lam_re.shape
    lr, li = lam_re.astype(F32), lam_im.astype(F32)
    dt = jnp.exp(log_dt.astype(F32))[:, None]
    ar, ai = lr * dt, li * dt
    steps = jnp.arange(t_len + 1, dtype=F32)[:, None, None]
    mag = jnp.exp(ar[None] * steps)
    pr, pi = mag * jnp.cos(ai[None] * steps), mag * jnp.sin(ai[None] * steps)
    nr, ni = pr[1] - 1.0, pi[1]
    den = lr * lr + li * li
    fr, fi = (nr * lr + ni * li) / den, (ni * lr - nr * li) / den
    bre, bim = b_re.astype(F32), b_im.astype(F32)
    bbr = fr[..., None] * bre - fi[..., None] * bim
    bbi = fr[..., None] * bim + fi[..., None] * bre
    cre, cim = c_re.astype(F32), c_im.astype(F32)
    cpr = cre[None] * pr[:, :, None, :] - cim[None] * pi[:, :, None, :]
    cpi = cre[None] * pi[:, :, None, :] + cim[None] * pr[:, :, None, :]
    taps = (jnp.einsum("dgip,gpj->gjid", cpr[:t_len], bbr, precision=hp)
            - jnp.einsum("dgip,gpj->gjid", cpi[:t_len], bbi, precision=hp))

    rev = t_len - 1 - jnp.arange(t_len)
    qr, qi = pr[rev][..., None], pi[rev][..., None]
    sr = qr * bbr[None] - qi * bbi[None]
    si = qr * bbi[None] + qi * bbr[None]
    sr = jnp.transpose(sr, (1, 3, 0, 2))
    si = jnp.transpose(si, (1, 3, 0, 2))
    wst = jnp.concatenate([sr, si, si, sr], axis=-1).reshape(
        groups, S5_GROUP * t_len, 4 * n_state)

    wc = jnp.concatenate([jnp.transpose(cpr[1:], (1, 3, 2, 0)),
                          -jnp.transpose(cpi[1:], (1, 3, 2, 0))], axis=1).reshape(
        groups, 2 * n_state, S5_GROUP * t_len)

    a_re, a_im = pr[t_len], pi[t_len]
    a1 = jnp.concatenate([a_re, a_re, a_re, a_re], axis=-1)[:, None, :]
    a2 = jnp.concatenate([-a_im, a_im, a_im, -a_im], axis=-1)[:, None, :]
    dsk = jnp.repeat(d_skip.astype(F32).reshape(groups, 1, S5_GROUP), t_len, axis=-1)
    return taps, wst.astype(BF16), wc.astype(BF16), a1, a2, dsk


def _mixer_kernel(x_ref, yst_ref, q_ref, k_ref, v_ref, o_ref, gs_ref, gm_ref,
                  g0_ref, b0_ref, wgc_ref, bgc_ref, wgr_ref, bgr_ref,
                  wglu_ref, mng_ref, wmo_ref, wo_ref, g1_ref, b1_ref,
                  c0_ref, n0_ref, m0_ref,
                  *refs, tile, n_pad, alpha, emit_state, d_model):
    if emit_state:
        out_ref, c_out_ref, n_out_ref, m_out_ref, c_ref, n_ref, m_ref, hm_ref = refs
    else:
        out_ref, c_ref, n_ref, m_ref, hm_ref = refs
    j = pl.program_id(1)
    dk = q_ref.shape[-1] // M_HEADS
    dv = v_ref.shape[-1] // M_HEADS

    @pl.when(j == 0)
    def _():
        c_ref[...] = c0_ref[...]
        n_ref[...] = n0_ref[...]
        m_ref[...] = m0_ref[...]

    h0 = _layer_norm(x_ref[...], g0_ref[...], b0_ref[...])
    hb = h0.astype(BF16)

    g_col = _dot(hb, wgc_ref[...]) + bgc_ref[...]
    g_row = _dot_nt(wgr_ref[...], hb) + bgr_ref[...]
    lane = lax.broadcasted_iota(jnp.int32, g_col.shape, 1)
    sub = lax.broadcasted_iota(jnp.int32, g_row.shape, 0)
    is_f_col = (lane >= M_HEADS) & (lane < 2 * M_HEADS)
    is_f_row = sub >= M_HEADS
    lf_col = jnp.where(is_f_col, _log_sigmoid(g_col), 0.0)
    lf_row = jnp.where(is_f_row, _log_sigmoid(g_row), 0.0)
    li_col = g_col
    li_row = g_row
    if n_pad:
        t_col = lax.broadcasted_iota(jnp.int32, g_col.shape, 0) + j * tile
        t_row = lax.broadcasted_iota(jnp.int32, g_row.shape, 1) + j * tile
        lf_col = jnp.where(t_col >= n_pad, lf_col, 0.0)
        lf_row = jnp.where(t_row >= n_pad, lf_row, 0.0)
        li_col = jnp.where(t_col >= n_pad, li_col, NEG_INF)
        li_row = jnp.where(t_row >= n_pad, li_row, NEG_INF)

    ti = lax.broadcasted_iota(jnp.int32, (tile, tile), 0)
    tj = lax.broadcasted_iota(jnp.int32, (tile, tile), 1)
    causal = ti >= tj
    tri_lower = jnp.where(causal, 1.0, 0.0).astype(BF16)
    tri_upper = jnp.where(ti <= tj, 1.0, 0.0).astype(BF16)
    b_col = sum(_dot(tri_lower, part) for part in _split3(lf_col))
    b_row = sum(_dot(part, tri_upper) for part in _split3(lf_row))

    for h in range(M_HEADS):
        qh = q_ref[:, h * dk:(h + 1) * dk]
        kh = k_ref[:, h * dk:(h + 1) * dk]
        vh = v_ref[:, h * dv:(h + 1) * dv]
        bc = b_col[:, M_HEADS + h:M_HEADS + h + 1]
        lic = li_col[:, h:h + 1]
        br = b_row[M_HEADS + h:M_HEADS + h + 1, :]
        lir = li_row[h:h + 1, :]
        m_st = m_ref[h:h + 1, 0:1]
        c_st = c_ref[h]
        n_st = n_ref[h:h + 1, :]

        d_mat = jnp.where(causal, bc - br + lir, NEG_INF)
        m_inter = bc + m_st
        m_row = jnp.maximum(m_inter, jnp.max(d_mat, axis=-1, keepdims=True))
        w_intra = jnp.exp(d_mat - m_row)
        w_inter = jnp.exp(m_inter - m_row)
        s_mat = _dot_nt(qh, kh) * w_intra
        num = _dot(s_mat.astype(BF16), vh) + w_inter * _dot(qh, c_st.astype(BF16))
        den = (jnp.sum(s_mat, axis=-1, keepdims=True)
               + w_inter * jnp.sum(qh.astype(F32) * n_st, axis=-1, keepdims=True))
        hh = num / jnp.maximum(jnp.abs(den), jnp.exp(-m_row))

        b_last = br[:, tile - 1:tile]
        g_col_log = b_last - bc + lic
        g_row_log = b_last - br + lir
        m_new = jnp.maximum(b_last + m_st, jnp.max(g_row_log, axis=-1, keepdims=True))
        w_k = jnp.exp(g_col_log - m_new)
        decay = jnp.exp(b_last + m_st - m_new)
        kw = kh.astype(F32) * w_k
        c_ref[h] = decay * c_st + _dot_tn(kw.astype(BF16), vh)
        n_ref[h:h + 1, :] = decay * n_st + jnp.sum(kw, axis=0, keepdims=True)
        m_ref[h:h + 1, :] = jnp.broadcast_to(m_new, (1, m_ref.shape[1]))

        mu = jnp.mean(hh, axis=-1, keepdims=True)
        hc = hh - mu
        var = jnp.mean(hc * hc, axis=-1, keepdims=True)
        hn = hc * lax.rsqrt(var + LN_EPS) * mng_ref[:, h * dv:(h + 1) * dv]
        hm_ref[:, h * dv:(h + 1) * dv] = (
            o_ref[:, h * dv:(h + 1) * dv].astype(F32) * hn).astype(BF16)

    y_m = _dot(hm_ref[...], wmo_ref[...])
    z = _dot_tn(_gelu_tanh(yst_ref[...].astype(F32)).astype(BF16), wglu_ref[...])
    y_s5 = z[:, :d_model] * jax.nn.sigmoid(z[:, d_model:])
    mix = gs_ref[...].astype(F32) * y_s5 + gm_ref[...].astype(F32) * y_m
    res = alpha * h0 + _dot(mix.astype(BF16), wo_ref[...])
    out_ref[...] = _layer_norm(res, g1_ref[...], b1_ref[...])
    if emit_state:
        c_out_ref[...] = c_ref[...]
        n_out_ref[...] = n_ref[...]
        m_out_ref[...] = m_ref[...]


def _mixer(x, yst, q, k, v, o, gs, gm, g0, b0, wgc, bgc, wgr, bgr, wglu, mng, wmo, wo, g1, b1,
           c0, n0, m0, *, n_pad, alpha, emit_state):
    bsz, length, d_model = x.shape
    tile = TILE_MIX
    assert length % tile == 0
    d_qk = q.shape[-1]
    d_v = v.shape[-1]
    d_s5 = yst.shape[1]
    dk, dv = d_qk // M_HEADS, d_v // M_HEADS
    grid = (bsz, length // tile)

    def tok(width):
        return pl.BlockSpec((None, tile, width), lambda b, j: (b, j, 0))

    consts = [g0, b0, wgc, bgc, wgr, bgr, wglu, mng, wmo, wo, g1, b1, c0, n0, m0]
    in_specs = ([tok(d_model), pl.BlockSpec((None, d_s5, tile), lambda b, j: (b, 0, j)),
                 tok(d_qk), tok(d_qk), tok(d_v), tok(d_v), tok(d_model), tok(d_model)]
                + [_const_spec(c.shape) for c in consts])
    out_specs = [tok(d_model)]
    out_shape = [jax.ShapeDtypeStruct((bsz, length, d_model), F32)]
    if emit_state:
        out_specs += [pl.BlockSpec((None, M_HEADS, dk, dv), lambda b, j: (b, 0, 0, 0)),
                      pl.BlockSpec((None, 8, dk), lambda b, j: (b, 0, 0)),
                      pl.BlockSpec((None, 8, dk), lambda b, j: (b, 0, 0))]
        out_shape += [jax.ShapeDtypeStruct((bsz, M_HEADS, dk, dv), F32),
                      jax.ShapeDtypeStruct((bsz, 8, dk), F32),
                      jax.ShapeDtypeStruct((bsz, 8, dk), F32)]
    kern = functools.partial(_mixer_kernel, tile=tile, n_pad=n_pad, alpha=alpha,
                             emit_state=emit_state, d_model=d_model)
    return pl.pallas_call(
        kern,
        grid=grid,
        in_specs=in_specs,
        out_specs=out_specs,
        out_shape=out_shape,
        scratch_shapes=[pltpu.VMEM((M_HEADS, dk, dv), F32), pltpu.VMEM((8, dk), F32),
                        pltpu.VMEM((8, dk), F32), pltpu.VMEM((tile, d_v), BF16)],
        compiler_params=pltpu.CompilerParams(
            dimension_semantics=("arbitrary", "arbitrary"), vmem_limit_bytes=V7X_VMEM_LIMIT),
        name="mixer",
    )(x, yst, q, k, v, o, gs, gm, *consts)


def _ffn_kernel(h_ref, wu_ref, bu_ref, wd_ref, g2_ref, b2_ref, out_ref, *, alpha, ff_chunk):
    h = h_ref[...]
    hb = h.astype(BF16)
    d_ff = wu_ref.shape[1]
    acc = alpha * h
    for c in range(d_ff // ff_chunk):
        lo = c * ff_chunk
        a = jnp.maximum(_dot(hb, wu_ref[:, lo:lo + ff_chunk]) + bu_ref[:, lo:lo + ff_chunk], 0.0)
        acc = acc + _dot((a * a).astype(BF16), wd_ref[lo:lo + ff_chunk, :])
    out_ref[...] = _layer_norm(acc, g2_ref[...], b2_ref[...])


def _ffn(h, wu, bu, wd, g2, b2, *, alpha):
    rows, d_model = h.shape
    tile = TILE_FFN
    assert rows % tile == 0
    d_ff = wu.shape[1]
    return pl.pallas_call(
        functools.partial(_ffn_kernel, alpha=alpha, ff_chunk=min(1024, d_ff)),
        grid=(rows // tile,),
        in_specs=[pl.BlockSpec((tile, d_model), lambda i: (i, 0)),
                  _const_spec((d_model, d_ff)), _const_spec((1, d_ff)),
                  _const_spec((d_ff, d_model)), _const_spec((1, d_model)),
                  _const_spec((1, d_model))],
        out_specs=pl.BlockSpec((tile, d_model), lambda i: (i, 0)),
        out_shape=jax.ShapeDtypeStruct((rows, d_model), F32),
        compiler_params=pltpu.CompilerParams(
            dimension_semantics=("arbitrary",), vmem_limit_bytes=V7X_VMEM_LIMIT),
        name="ffn",
    )(h, wu, bu, wd, g2, b2)


def kernel(x, meta_tokens, ln0_g, ln0_b, w_in, b_in, qk_conv_w, qk_conv_b, s5_lambda_re, s5_lambda_im, s5_log_dt, s5_b_re, s5_b_im, s5_c_re, s5_c_im, s5_d, s5_w_glu, m_norm_g, m_w_out, w_o, ln1_g, ln1_b, w_up, b_up, w_down, ln2_g, ln2_b):
    bsz, seq, d_model = x.shape
    depth = w_in.shape[0]
    assert depth == 1, "the meta-token prologue is written for a single layer"
    assert meta_tokens.shape == (N_META, d_model)
    assert seq % TILE_MIX == 0 and seq % S5_CHUNK == 0 and TILE_MIX >= S5_CHUNK >= N_META
    alpha = (2.0 * depth) ** 0.25
    d_s5 = d_model // 2
    d_qk = qk_conv_w.shape[-1] // 2
    d_v = m_norm_g.shape[-1]
    dk, dv = d_qk // M_HEADS, d_v // M_HEADS
    gate_lo = d_s5 + 2 * d_qk + 2 * d_v
    gate_hi = gate_lo + 2 * M_HEADS
    assert w_in.shape[-1] == gate_hi + 2 * d_model

    row = lambda a: a.reshape(1, -1).astype(F32)
    w = w_in[0]
    bias = b_in[0]
    w_ut = w[:, :d_s5].T.astype(BF16)
    b_ut = bias[:d_s5].reshape(d_s5, 1).astype(F32)
    w_main = jnp.concatenate([w[:, d_s5:gate_lo], w[:, gate_hi:]], axis=1).astype(BF16)
    b_main = row(jnp.concatenate([bias[d_s5:gate_lo], bias[gate_hi:]]))
    w_gate = w[:, gate_lo:gate_hi]
    wgc = jnp.pad(w_gate, ((0, 0), (0, LANES - 2 * M_HEADS))).astype(BF16)
    bgc = jnp.pad(bias[gate_lo:gate_hi], (0, LANES - 2 * M_HEADS)).reshape(1, LANES).astype(F32)
    wgr = w_gate.T.astype(BF16)
    bgr = bias[gate_lo:gate_hi].reshape(2 * M_HEADS, 1).astype(F32)
    g0, b0 = row(ln0_g), row(ln0_b)
    conv_w = qk_conv_w[0].astype(F32)
    conv_b = row(qk_conv_b[0])
    mixer_consts = (g0, b0, wgc, bgc, wgr, bgr, s5_w_glu[0].astype(BF16), row(m_norm_g[0]),
                    m_w_out[0].astype(BF16), w_o[0].astype(BF16), row(ln1_g[0]), row(ln1_b[0]))
    inproj_consts = (g0, b0, w_ut, b_ut, w_main, b_main, conv_w, conv_b)

    pad = TILE_MIX - N_META
    x_meta = jnp.concatenate([jnp.zeros((pad, d_model), x.dtype), meta_tokens.astype(x.dtype)])[None]
    ut_m, q_m, k_m, v_m, o_m, gs_m, gm_m, carry_m = _inproj(
        x_meta, *inproj_consts, jnp.zeros((CARRY_ROWS, 2 * d_qk), F32), n_pad=pad)
    zero_state = (jnp.zeros((M_HEADS, dk, dv), F32), jnp.zeros((8, dk), F32),
                  jnp.zeros((8, dk), F32))
    _, c_m, n_m, m_m = _mixer(
        x_meta, jnp.zeros((1, d_s5, TILE_MIX), BF16), q_m, k_m, v_m, o_m, gs_m, gm_m,
        *mixer_consts, *zero_state, n_pad=pad, alpha=alpha, emit_state=True)

    ut, q, k, v, o, gs, gm, _ = _inproj(x, *inproj_consts, carry_m[0], n_pad=0)
    s5_consts = _s5_weights(s5_lambda_re[0], s5_lambda_im[0], s5_log_dt[0], s5_b_re[0], s5_b_im[0],
                            s5_c_re[0], s5_c_im[0], s5_d[0])
    yst = _s5(ut, ut_m[0, :, TILE_MIX - S5_CHUNK:], *s5_consts)
    (h1,) = _mixer(x, yst, q, k, v, o, gs, gm, *mixer_consts, c_m[0], n_m[0], m_m[0],
                   n_pad=0, alpha=alpha, emit_state=False)
    out = _ffn(h1.reshape(bsz * seq, d_model), w_up[0].astype(BF16), row(b_up[0]),
               w_down[0].astype(BF16), row(ln2_g[0]), row(ln2_b[0]), alpha=alpha)
    return out.reshape(bsz, seq, d_model)
```

```python
import functools
import math

import jax
import jax.numpy as jnp
from jax import lax
from jax.experimental import pallas as pl
from jax.experimental.pallas import tpu as pltpu

F32 = jnp.float32
BF16 = jnp.bfloat16

N_META = 16
S5_GROUP = 16
S5_STATE = 64
M_HEADS = 4
CONV_WIDTH = 4
LN_EPS = 1e-5

LANES = 128
S5_CHUNK = LANES
TILE_IN = 512
TILE_MIX = 256
TILE_FFN = 512
ROW_BLOCK = 256
CARRY_ROWS = 8
V7X_VMEM_LIMIT = 56 * 1024 * 1024

NEG_INF = float("-inf")


def _layer_norm(x, g, b):
    mu = jnp.mean(x, axis=-1, keepdims=True)
    xc = x - mu
    var = jnp.mean(xc * xc, axis=-1, keepdims=True)
    return xc * lax.rsqrt(var + LN_EPS) * g + b


def _log_sigmoid(x):
    return jnp.minimum(x, 0.0) - jnp.log1p(jnp.exp(-jnp.abs(x)))


def _gelu_tanh(x):
    c = math.sqrt(2.0 / math.pi)
    return x * (0.5 * (1.0 + jnp.tanh(c * (x + 0.044715 * (x * x * x)))))


def _dot(a, b):
    return jnp.dot(a, b, preferred_element_type=F32)


def _dot_nt(a, b):
    return lax.dot_general(a, b, (((1,), (1,)), ((), ())), preferred_element_type=F32)


def _dot_tn(a, b):
    return lax.dot_general(a, b, (((0,), (0,)), ((), ())), preferred_element_type=F32)


def _split3(x):
    hi = x.astype(BF16)
    r = x - hi.astype(F32)
    mid = r.astype(BF16)
    lo = (r - mid.astype(F32)).astype(BF16)
    return hi, mid, lo


def _const_spec(shape):
    nd = len(shape)
    return pl.BlockSpec(shape, lambda *_: (0,) * nd, pipeline_mode=pl.Buffered(1))


def _inproj_kernel(x_ref, g0_ref, b0_ref, wt_ref, bt_ref, w_ref, b_ref, cw_ref, cb_ref,
                   wgc_ref, bgc_ref, wgr_ref, bgr_ref, carry_in_ref,
                   ut_ref, q_ref, k_ref, vt_ref, ot_ref, gst_ref, gmt_ref, gcol_ref, grow_ref,
                   carry_out_ref,
                   ext_ref, *, tile, row_block, n_pad, d_qk, d_s5, d_v, d_model):
    j = pl.program_id(1)

    @pl.when(j == 0)
    def _():
        ext_ref[0:CARRY_ROWS, :] = carry_in_ref[...]

    blocks = [slice(r0, r0 + row_block) for r0 in range(0, tile, row_block)]
    hbs = []
    for rows in blocks:
        hb = _layer_norm(x_ref[rows, :], g0_ref[...], b0_ref[...]).astype(BF16)
        hbs.append(hb)
        qk = _dot(hb, w_ref[...]) + b_ref[...]
        if n_pad:
            t_idx = lax.broadcasted_iota(jnp.int32, (row_block, 1), 0) + (j * tile + rows.start)
            qk = jnp.where(t_idx >= n_pad, qk, 0.0)
        ext_ref[CARRY_ROWS + rows.start:CARRY_ROWS + rows.stop, :] = qk
        gcol_ref[rows, :] = _dot(hb, wgc_ref[...]) + bgc_ref[...]
        grow_ref[:, rows] = _dot_nt(wgr_ref[...], hb) + bgr_ref[...]

    conv = cb_ref[...] + cw_ref[CONV_WIDTH - 1:CONV_WIDTH, :] * ext_ref[pl.ds(CARRY_ROWS, tile), :]
    for r in range(CONV_WIDTH - 1):
        lag = CONV_WIDTH - 1 - r
        conv = conv + cw_ref[r:r + 1, :] * ext_ref[pl.ds(CARRY_ROWS - lag, tile), :]
    tail = ext_ref[pl.ds(tile, CARRY_ROWS), :]
    ext_ref[0:CARRY_ROWS, :] = tail
    carry_out_ref[...] = tail
    act = conv * jax.nn.sigmoid(conv)
    q_ref[...] = act[:, :d_qk].astype(BF16)
    k_ref[...] = (act[:, d_qk:] * ((d_qk // M_HEADS) ** -0.5)).astype(BF16)

    def seg_t(lo, n_rows, rows, hb):
        return _dot_nt(wt_ref[lo:lo + n_rows, :], hb) + bt_ref[lo:lo + n_rows, :]

    for rows, hb in zip(blocks, hbs):
        ut = seg_t(0, d_s5, rows, hb)
        if n_pad:
            cols = lax.broadcasted_iota(jnp.int32, (1, row_block), 1) + (j * tile + rows.start)
            ut = jnp.where(cols >= n_pad, ut, 0.0)
        ut_ref[:, rows] = ut.astype(BF16)
    off = d_s5
    for rows, hb in zip(blocks, hbs):
        vt_ref[:, rows] = seg_t(off, d_v, rows, hb).astype(BF16)
    off += d_v
    for out_ref, n_rows in ((ot_ref, d_v), (gst_ref, d_model), (gmt_ref, d_model)):
        for rows, hb in zip(blocks, hbs):
            out_ref[:, rows] = jax.nn.sigmoid(seg_t(off, n_rows, rows, hb)).astype(BF16)
        off += n_rows


def _inproj(x, g0, b0, w_t, b_t, w_qk, b_qk, conv_w, conv_b, wgc, bgc, wgr, bgr, carry_in,
            *, n_pad):
    bsz, length, d_model = x.shape
    tile = min(TILE_IN, length)
    assert length % tile == 0
    d_qk = conv_w.shape[1] // 2
    d_s5 = d_model // 2
    d_v = d_model
    n_t = w_t.shape[0]
    assert n_t == d_s5 + 2 * d_v + 2 * d_model and w_qk.shape[1] == 2 * d_qk
    grid = (bsz, length // tile)

    def tok(width):
        return pl.BlockSpec((None, tile, width), lambda b, j: (b, j, 0))

    def feat(rows):
        return pl.BlockSpec((None, rows, tile), lambda b, j: (b, 0, j))

    def feat_out(rows):
        return jax.ShapeDtypeStruct((bsz, rows, length), BF16)

    kern = functools.partial(_inproj_kernel, tile=tile, row_block=min(ROW_BLOCK, tile), n_pad=n_pad,
                             d_qk=d_qk, d_s5=d_s5, d_v=d_v, d_model=d_model)
    return pl.pallas_call(
        kern,
        grid=grid,
        in_specs=[tok(d_model), _const_spec((1, d_model)), _const_spec((1, d_model)),
                  _const_spec((n_t, d_model)), _const_spec((n_t, 1)),
                  _const_spec((d_model, 2 * d_qk)), _const_spec((1, 2 * d_qk)),
                  _const_spec((CONV_WIDTH, 2 * d_qk)), _const_spec((1, 2 * d_qk)),
                  _const_spec(wgc.shape), _const_spec(bgc.shape), _const_spec(wgr.shape),
                  _const_spec(bgr.shape), _const_spec((CARRY_ROWS, 2 * d_qk))],
        out_specs=[feat(d_s5), tok(d_qk), tok(d_qk), feat(d_v), feat(d_v), feat(d_model),
                   feat(d_model), tok(LANES), feat(2 * M_HEADS),
                   pl.BlockSpec((None, CARRY_ROWS, 2 * d_qk), lambda b, j: (b, 0, 0))],
        out_shape=[feat_out(d_s5),
                   jax.ShapeDtypeStruct((bsz, length, d_qk), BF16),
                   jax.ShapeDtypeStruct((bsz, length, d_qk), BF16),
                   feat_out(d_v), feat_out(d_v), feat_out(d_model), feat_out(d_model),
                   jax.ShapeDtypeStruct((bsz, length, LANES), F32),
                   jax.ShapeDtypeStruct((bsz, 2 * M_HEADS, length), F32),
                   jax.ShapeDtypeStruct((bsz, CARRY_ROWS, 2 * d_qk), F32)],
        scratch_shapes=[pltpu.VMEM((CARRY_ROWS + tile, 2 * d_qk), F32)],
        compiler_params=pltpu.CompilerParams(
            dimension_semantics=("arbitrary", "arbitrary"), vmem_limit_bytes=V7X_VMEM_LIMIT),
        name="inproj",
    )(x, g0, b0, w_t, b_t, w_qk, b_qk, conv_w, conv_b, wgc, bgc, wgr, bgr, carry_in)


def _s5_kernel(ut_ref, um_ref, taps_ref, wst_ref, wc_ref, a1_ref, a2_ref, dsk_ref, yt_ref,
               px_ref, lhs_ref, toe_ref, s_ref, xin_ref, y_ref, ymid_ref, *, n_chunks, bsz):
    t_len = S5_CHUNK
    n_ch = S5_GROUP
    length = n_chunks * t_len
    half = 2 * S5_STATE
    rows_bj = bsz * n_ch
    lane_step = 1024

    r_idx = lax.broadcasted_iota(jnp.int32, (rows_bj, rows_bj), 0)
    q_idx = lax.broadcasted_iota(jnp.int32, (rows_bj, rows_bj), 1)
    perm = jnp.where(q_idx == (r_idx % bsz) * n_ch + r_idx // bsz, 1.0, 0.0).astype(BF16)
    perm_back = jnp.where(q_idx == (r_idx % n_ch) * bsz + r_idx // n_ch, 1.0, 0.0).astype(BF16)
    x_bj = ut_ref[...].reshape(rows_bj, length)
    for lo in range(0, length, lane_step):
        px_ref[:, lo:lo + lane_step] = _dot(perm, x_bj[:, lo:lo + lane_step]).astype(BF16)

    for jj in range(n_ch):
        lhs_ref[0:bsz, jj * t_len:(jj + 1) * t_len] = jnp.broadcast_to(
            um_ref[jj:jj + 1, :], (bsz, t_len))
    for c in range(n_chunks):
        for jj in range(n_ch):
            lhs_ref[(c + 1) * bsz:(c + 2) * bsz, jj * t_len:(jj + 1) * t_len] = (
                px_ref[jj * bsz:(jj + 1) * bsz, c * t_len:(c + 1) * t_len])

    s_io = lax.broadcasted_iota(jnp.int32, (t_len, t_len), 0)
    t_io = lax.broadcasted_iota(jnp.int32, (t_len, t_len), 1)
    lower = t_io >= s_io

    def toe_rows(jj, carry):
        r0 = pl.multiple_of(jj * t_len, t_len)
        for ii in range(n_ch):
            tap = jnp.broadcast_to(taps_ref[jj, ii:ii + 1, :], (t_len, t_len))
            blk = pltpu.roll(tap, 0, 1, stride=1, stride_axis=0)
            toe_ref[pl.ds(r0, t_len), ii * t_len:(ii + 1) * t_len] = (
                jnp.where(lower, blk, 0.0).astype(BF16))
        return carry

    lax.fori_loop(0, n_ch, toe_rows, 0)

    lhs = lhs_ref[...]
    s_ref[...] = _dot(lhs, wst_ref[...])
    a1 = a1_ref[...]
    a2 = a2_ref[...]

    def step(c, state):
        r = pl.multiple_of(c * bsz, bsz)
        xin_ref[pl.ds(r, bsz), :] = state[:, :half]
        swapped = jnp.concatenate([state[:, half:], state[:, :half]], axis=1)
        return a1 * state + a2 * swapped + s_ref[pl.ds(r, bsz), :]

    lax.fori_loop(0, n_chunks + 1, step, jnp.zeros((bsz, 2 * half), F32))
    y_ref[...] = (_dot(lhs, toe_ref[...]) + _dot(xin_ref[...].astype(BF16), wc_ref[...])
                  + lhs.astype(F32) * dsk_ref[...])

    for c in range(n_chunks):
        for ii in range(n_ch):
            ymid_ref[ii * bsz:(ii + 1) * bsz, c * t_len:(c + 1) * t_len] = (
                y_ref[(c + 1) * bsz:(c + 2) * bsz, ii * t_len:(ii + 1) * t_len].astype(BF16))
    for lo in range(0, length, lane_step):
        yt_ref[:, :, lo:lo + lane_step] = _dot(
            perm_back, ymid_ref[:, lo:lo + lane_step]).astype(BF16).reshape(bsz, n_ch, lane_step)


def _s5(ut, um, taps, wst, wc, a1, a2, dskip):
    bsz, d_s5, length = ut.shape
    groups = d_s5 // S5_GROUP
    t_len = S5_CHUNK
    n_chunks = length // t_len
    assert length % 1024 == 0
    width = S5_GROUP * t_len
    half = 2 * S5_STATE
    rows = (n_chunks + 1) * bsz

    def grp(*shape):
        return pl.BlockSpec((None,) + shape, lambda g: (g,) + (0,) * len(shape))

    return pl.pallas_call(
        functools.partial(_s5_kernel, n_chunks=n_chunks, bsz=bsz),
        grid=(groups,),
        in_specs=[pl.BlockSpec((bsz, S5_GROUP, length), lambda g: (0, g, 0)),
                  pl.BlockSpec((S5_GROUP, t_len), lambda g: (g, 0)),
                  grp(S5_GROUP, S5_GROUP, t_len), grp(width, 2 * half), grp(half, width),
                  grp(1, 2 * half), grp(1, 2 * half), grp(1, width)],
        out_specs=pl.BlockSpec((bsz, S5_GROUP, length), lambda g: (0, g, 0)),
        out_shape=jax.ShapeDtypeStruct((bsz, d_s5, length), BF16),
        scratch_shapes=[pltpu.VMEM((bsz * S5_GROUP, length), BF16),
                        pltpu.VMEM((rows, width), BF16),
                        pltpu.VMEM((width, width), BF16),
                        pltpu.VMEM((rows, 2 * half), F32),
                        pltpu.VMEM((rows, half), F32),
                        pltpu.VMEM((rows, width), F32),
                        pltpu.VMEM((bsz * S5_GROUP, length), BF16)],
        compiler_params=pltpu.CompilerParams(
            dimension_semantics=("arbitrary",), vmem_limit_bytes=V7X_VMEM_LIMIT),
        name="s5",
    )(ut, um, taps, wst, wc, a1, a2, dskip)


def _s5_weights(lam_re, lam_im, log_dt, b_re, b_im, c_re, c_im, d_skip):
    hp = lax.Precision.HIGHEST
    t_len = S5_CHUNK
    groups, n_state = lam_re.shape
    lr, li = lam_re.astype(F32), lam_im.astype(F32)
    dt = jnp.exp(log_dt.astype(F32))[:, None]
    ar, ai = lr * dt, li * dt
    steps = jnp.arange(t_len + 1, dtype=F32)[:, None, None]
    mag = jnp.exp(ar[None] * steps)
    pr, pi = mag * jnp.cos(ai[None] * steps), mag * jnp.sin(ai[None] * steps)
    nr, ni = pr[1] - 1.0, pi[1]
    den = lr * lr + li * li
    fr, fi = (nr * lr + ni * li) / den, (ni * lr - nr * li) / den
    bre, bim = b_re.astype(F32), b_im.astype(F32)
    bbr = fr[..., None] * bre - fi[..., None] * bim
    bbi = fr[..., None] * bim + fi[..., None] * bre
    cre, cim = c_re.astype(F32), c_im.astype(F32)
    cpr = cre[None] * pr[:, :, None, :] - cim[None] * pi[:, :, None, :]
    cpi = cre[None] * pi[:, :, None, :] + cim[None] * pr[:, :, None, :]
    taps = (jnp.einsum("dgip,gpj->gjid", cpr[:t_len], bbr, precision=hp)
            - jnp.einsum("dgip,gpj->gjid", cpi[:t_len], bbi, precision=hp))

    rev = t_len - 1 - jnp.arange(t_len)
    qr, qi = pr[rev][..., None], pi[rev][..., None]
    sr = qr * bbr[None] - qi * bbi[None]
    si = qr * bbi[None] + qi * bbr[None]
    sr = jnp.transpose(sr, (1, 3, 0, 2))
    si = jnp.transpose(si, (1, 3, 0, 2))
    wst = jnp.concatenate([sr, si, si, sr], axis=-1).reshape(
        groups, S5_GROUP * t_len, 4 * n_state)

    wc = jnp.concatenate([jnp.transpose(cpr[1:], (1, 3, 2, 0)),
                          -jnp.transpose(cpi[1:], (1, 3, 2, 0))], axis=1).reshape(
        groups, 2 * n_state, S5_GROUP * t_len)

    a_re, a_im = pr[t_len], pi[t_len]
    a1 = jnp.concatenate([a_re, a_re, a_re, a_re], axis=-1)[:, None, :]
    a2 = jnp.concatenate([-a_im, a_im, a_im, -a_im], axis=-1)[:, None, :]
    dsk = jnp.repeat(d_skip.astype(F32).reshape(groups, 1, S5_GROUP), t_len, axis=-1)
    return taps, wst.astype(BF16), wc.astype(BF16), a1, a2, dsk


def _mixer_kernel(yst_ref, q_ref, k_ref, vt_ref, ot_ref, gst_ref, gmt_ref, gcol_ref, grow_ref,
                  wglut_ref, mngt_ref, wmot_ref, wo_ref, c0_ref, n0_ref, m0_ref,
                  *refs, tile, n_pad, emit_state, d_model):
    if emit_state:
        out_ref, c_out_ref, n_out_ref, m_out_ref, ct_ref, n_ref, m_ref, hmt_ref = refs
    else:
        out_ref, ct_ref, n_ref, m_ref, hmt_ref = refs
    j = pl.program_id(1)
    dk = q_ref.shape[-1] // M_HEADS
    dv = vt_ref.shape[0] // M_HEADS

    @pl.when(j == 0)
    def _():
        ct_ref[...] = c0_ref[...]
        n_ref[...] = n0_ref[...]
        m_ref[...] = m0_ref[...]

    heads = range(M_HEADS)
    qs = [q_ref[:, h * dk:(h + 1) * dk] for h in heads]
    ks = [k_ref[:, h * dk:(h + 1) * dk] for h in heads]
    vts = [vt_ref[h * dv:(h + 1) * dv, :] for h in heads]
    m_sts = [m_ref[h:h + 1, 0:1] for h in heads]
    ct_sts = [ct_ref[h] for h in heads]
    n_sts = [n_ref[h] for h in heads]
    kq = [_dot_nt(ks[h], qs[h]) for h in heads]
    cq = [_dot_nt(ct_sts[h].astype(BF16), qs[h]) for h in heads]
    nq = [_dot_nt(n_sts[h].astype(BF16), qs[h])[0:1, :] for h in heads]
    zt = _dot(wglut_ref[...], _gelu_tanh(yst_ref[...].astype(F32)).astype(BF16))

    g_col = gcol_ref[...]
    g_row = grow_ref[...]
    lane = lax.broadcasted_iota(jnp.int32, g_col.shape, 1)
    sub = lax.broadcasted_iota(jnp.int32, g_row.shape, 0)
    is_f_col = (lane >= M_HEADS) & (lane < 2 * M_HEADS)
    is_f_row = sub >= M_HEADS
    lf_col = jnp.where(is_f_col, _log_sigmoid(g_col), 0.0)
    lf_row = jnp.where(is_f_row, _log_sigmoid(g_row), 0.0)
    li_col = g_col
    li_row = g_row
    if n_pad:
        t_col = lax.broadcasted_iota(jnp.int32, g_col.shape, 0) + j * tile
        t_row = lax.broadcasted_iota(jnp.int32, g_row.shape, 1) + j * tile
        lf_col = jnp.where(t_col >= n_pad, lf_col, 0.0)
        lf_row = jnp.where(t_row >= n_pad, lf_row, 0.0)
        li_col = jnp.where(t_col >= n_pad, li_col, NEG_INF)
        li_row = jnp.where(t_row >= n_pad, li_row, NEG_INF)

    ti = lax.broadcasted_iota(jnp.int32, (tile, tile), 0)
    tj = lax.broadcasted_iota(jnp.int32, (tile, tile), 1)
    causal = ti >= tj
    tri_lower = jnp.where(causal, 1.0, 0.0).astype(BF16)
    tri_upper = jnp.where(ti <= tj, 1.0, 0.0).astype(BF16)
    b_col = sum(_dot(tri_lower, part) for part in _split3(lf_col))
    b_row = sum(_dot(part, tri_upper) for part in _split3(lf_row))

    brs = [b_row[M_HEADS + h:M_HEADS + h + 1, :] for h in heads]
    r_cols = [li_col[:, h:h + 1] - b_col[:, M_HEADS + h:M_HEADS + h + 1] for h in heads]
    r_rows = [li_row[h:h + 1, :] - brs[h] for h in heads]

    a_last = [jnp.maximum(m_sts[h], jnp.max(r_rows[h], axis=-1, keepdims=True)) for h in heads]
    w_k = [jnp.exp(r_rows[h] - a_last[h]) for h in heads]
    upd = [_dot((vts[h].astype(F32) * w_k[h]).astype(BF16), ks[h]) for h in heads]
    n_upd = [_dot(jnp.broadcast_to(w_k[h], (8, tile)).astype(BF16), ks[h]) for h in heads]
    for h in heads:
        decay = jnp.exp(m_sts[h] - a_last[h])
        ct_ref[h] = decay * ct_sts[h] + upd[h]
        n_ref[h] = decay * n_sts[h] + n_upd[h]
        m_ref[h:h + 1, :] = jnp.broadcast_to(brs[h][:, tile - 1:tile] + a_last[h],
                                             (1, m_ref.shape[1]))

    e_mats = [jnp.where(ti <= tj, r_cols[h], NEG_INF) for h in heads]
    a_rows = [jnp.maximum(m_sts[h], jnp.max(e_mats[h], axis=0, keepdims=True)) for h in heads]
    w_inters = [jnp.exp(m_sts[h] - a_rows[h]) for h in heads]
    s_mats = [kq[h] * jnp.exp(e_mats[h] - a_rows[h]) for h in heads]
    vs_t = [_dot(vts[h], s_mats[h].astype(BF16)) for h in heads]

    for h in heads:
        den = jnp.sum(s_mats[h], axis=0, keepdims=True) + w_inters[h] * nq[h]
        scale = 1.0 / jnp.maximum(jnp.abs(den), jnp.exp(-(brs[h] + a_rows[h])))
        hh = (vs_t[h] + w_inters[h] * cq[h]) * scale
        hc = hh - jnp.mean(hh, axis=0, keepdims=True)
        var = jnp.mean(hc * hc, axis=0, keepdims=True)
        hn = hc * lax.rsqrt(var + LN_EPS) * mngt_ref[h * dv:(h + 1) * dv, :]
        hmt_ref[h * dv:(h + 1) * dv, :] = (
            ot_ref[h * dv:(h + 1) * dv, :].astype(F32) * hn).astype(BF16)

    y_s5 = zt[:d_model, :] * jax.nn.sigmoid(zt[d_model:, :])
    y_m = _dot(wmot_ref[...], hmt_ref[...])
    mix = gst_ref[...].astype(F32) * y_s5 + gmt_ref[...].astype(F32) * y_m
    out_ref[...] = _dot_tn(mix.astype(BF16), wo_ref[...])
    if emit_state:
        c_out_ref[...] = ct_ref[...]
        n_out_ref[...] = n_ref[...]
        m_out_ref[...] = m_ref[...]


def _mixer(yst, q, k, vt, ot, gst, gmt, gcol, grow, wglut, mngt, wmot, wo, c0, n0, m0,
           *, n_pad, emit_state):
    bsz, d_model, length = gst.shape
    tile = TILE_MIX
    assert length % tile == 0 and mngt.shape[1] == tile
    d_qk = q.shape[-1]
    d_v = vt.shape[1]
    d_s5 = yst.shape[1]
    dk, dv = d_qk // M_HEADS, d_v // M_HEADS
    grid = (bsz, length // tile)

    def tok(width):
        return pl.BlockSpec((None, tile, width), lambda b, j: (b, j, 0))

    def feat(rows):
        return pl.BlockSpec((None, rows, tile), lambda b, j: (b, 0, j))

    consts = [wglut, mngt, wmot, wo, c0, n0, m0]
    in_specs = ([feat(d_s5), tok(d_qk), tok(d_qk), feat(d_v), feat(d_v), feat(d_model),
                 feat(d_model), tok(LANES), feat(2 * M_HEADS)]
                + [_const_spec(c.shape) for c in consts])
    out_specs = [tok(d_model)]
    out_shape = [jax.ShapeDtypeStruct((bsz, length, d_model), F32)]
    if emit_state:
        out_specs += [pl.BlockSpec((None, M_HEADS, dv, dk), lambda b, j: (b, 0, 0, 0)),
                      pl.BlockSpec((None, M_HEADS, 8, dk), lambda b, j: (b, 0, 0, 0)),
                      pl.BlockSpec((None, 8, dk), lambda b, j: (b, 0, 0))]
        out_shape += [jax.ShapeDtypeStruct((bsz, M_HEADS, dv, dk), F32),
                      jax.ShapeDtypeStruct((bsz, M_HEADS, 8, dk), F32),
                      jax.ShapeDtypeStruct((bsz, 8, dk), F32)]
    kern = functools.partial(_mixer_kernel, tile=tile, n_pad=n_pad, emit_state=emit_state,
                             d_model=d_model)
    return pl.pallas_call(
        kern,
        grid=grid,
        in_specs=in_specs,
        out_specs=out_specs,
        out_shape=out_shape,
        scratch_shapes=[pltpu.VMEM((M_HEADS, dv, dk), F32), pltpu.VMEM((M_HEADS, 8, dk), F32),
                        pltpu.VMEM((8, dk), F32), pltpu.VMEM((d_v, tile), BF16)],
        compiler_params=pltpu.CompilerParams(
            dimension_semantics=("arbitrary", "arbitrary"), vmem_limit_bytes=V7X_VMEM_LIMIT),
        name="mixer",
    )(yst, q, k, vt, ot, gst, gmt, gcol, grow, *consts)


def _ffn_kernel(x_ref, pre_ref, g0_ref, b0_ref, g1_ref, b1_ref, wu_ref, bu_ref, wd_ref, g2_ref,
                b2_ref, out_ref, *, alpha, ff_chunk, row_block):
    d_ff = wu_ref.shape[1]
    tile = x_ref.shape[0]
    for r0 in range(0, tile, row_block):
        rows = slice(r0, r0 + row_block)
        h0 = _layer_norm(x_ref[rows, :], g0_ref[...], b0_ref[...])
        h = _layer_norm(alpha * h0 + pre_ref[rows, :], g1_ref[...], b1_ref[...])
        hb = h.astype(BF16)
        acc = alpha * h
        for c in range(d_ff // ff_chunk):
            lo = c * ff_chunk
            a = jnp.maximum(
                _dot(hb, wu_ref[:, lo:lo + ff_chunk]) + bu_ref[:, lo:lo + ff_chunk], 0.0)
            acc = acc + _dot((a * a).astype(BF16), wd_ref[lo:lo + ff_chunk, :])
        out_ref[rows, :] = _layer_norm(acc, g2_ref[...], b2_ref[...])


def _ffn(x, pre, g0, b0, g1, b1, wu, bu, wd, g2, b2, *, alpha):
    rows, d_model = x.shape
    tile = TILE_FFN
    assert rows % tile == 0
    d_ff = wu.shape[1]
    return pl.pallas_call(
        functools.partial(_ffn_kernel, alpha=alpha, ff_chunk=min(1024, d_ff),
                          row_block=min(ROW_BLOCK, tile)),
        grid=(rows // tile,),
        in_specs=[pl.BlockSpec((tile, d_model), lambda i: (i, 0)),
                  pl.BlockSpec((tile, d_model), lambda i: (i, 0)),
                  _const_spec((1, d_model)), _const_spec((1, d_model)),
                  _const_spec((1, d_model)), _const_spec((1, d_model)),
                  _const_spec((d_model, d_ff)), _const_spec((1, d_ff)),
                  _const_spec((d_ff, d_model)), _const_spec((1, d_model)),
                  _const_spec((1, d_model))],
        out_specs=pl.BlockSpec((tile, d_model), lambda i: (i, 0)),
        out_shape=jax.ShapeDtypeStruct((rows, d_model), F32),
        compiler_params=pltpu.CompilerParams(
            dimension_semantics=("arbitrary",), vmem_limit_bytes=V7X_VMEM_LIMIT),
        name="ffn",
    )(x, pre, g0, b0, g1, b1, wu, bu, wd, g2, b2)


def kernel(x, meta_tokens, ln0_g, ln0_b, w_in, b_in, qk_conv_w, qk_conv_b, s5_lambda_re, s5_lambda_im, s5_log_dt, s5_b_re, s5_b_im, s5_c_re, s5_c_im, s5_d, s5_w_glu, m_norm_g, m_w_out, w_o, ln1_g, ln1_b, w_up, b_up, w_down, ln2_g, ln2_b):
    bsz, seq, d_model = x.shape
    depth = w_in.shape[0]
    assert depth == 1, "the meta-token prologue is written for a single layer"
    assert meta_tokens.shape == (N_META, d_model)
    assert seq % TILE_MIX == 0 and seq % S5_CHUNK == 0 and TILE_MIX >= S5_CHUNK >= N_META
    alpha = (2.0 * depth) ** 0.25
    d_s5 = d_model // 2
    d_qk = qk_conv_w.shape[-1] // 2
    d_v = m_norm_g.shape[-1]
    dk, dv = d_qk // M_HEADS, d_v // M_HEADS
    gate_lo = d_s5 + 2 * d_qk + 2 * d_v
    gate_hi = gate_lo + 2 * M_HEADS
    assert w_in.shape[-1] == gate_hi + 2 * d_model

    row = lambda a: a.reshape(1, -1).astype(F32)
    w = w_in[0]
    bias = b_in[0]
    qk_lo, qk_hi = d_s5, d_s5 + 2 * d_qk
    w_t = jnp.concatenate([w[:, :qk_lo], w[:, qk_hi:gate_lo], w[:, gate_hi:]], axis=1).T.astype(BF16)
    b_t = jnp.concatenate([bias[:qk_lo], bias[qk_hi:gate_lo], bias[gate_hi:]]).reshape(-1, 1).astype(F32)
    w_qk = w[:, qk_lo:qk_hi].astype(BF16)
    b_qk = row(bias[qk_lo:qk_hi])
    w_gate = w[:, gate_lo:gate_hi]
    wgc = jnp.pad(w_gate, ((0, 0), (0, LANES - 2 * M_HEADS))).astype(BF16)
    bgc = jnp.pad(bias[gate_lo:gate_hi], (0, LANES - 2 * M_HEADS)).reshape(1, LANES).astype(F32)
    wgr = w_gate.T.astype(BF16)
    bgr = bias[gate_lo:gate_hi].reshape(2 * M_HEADS, 1).astype(F32)
    g0, b0 = row(ln0_g), row(ln0_b)
    conv_w = qk_conv_w[0].astype(F32)
    conv_b = row(qk_conv_b[0])
    mngt = jnp.broadcast_to(m_norm_g[0].astype(F32)[:, None], (d_v, TILE_MIX))
    mixer_consts = (s5_w_glu[0].T.astype(BF16), mngt, m_w_out[0].T.astype(BF16),
                    w_o[0].astype(BF16))
    inproj_consts = (g0, b0, w_t, b_t, w_qk, b_qk, conv_w, conv_b, wgc, bgc, wgr, bgr)

    pad = TILE_MIX - N_META
    x_meta = jnp.concatenate([jnp.zeros((pad, d_model), x.dtype), meta_tokens.astype(x.dtype)])[None]
    ut_m, q_m, k_m, v_m, o_m, gs_m, gm_m, gcol_m, grow_m, carry_m = _inproj(
        x_meta, *inproj_consts, jnp.zeros((CARRY_ROWS, 2 * d_qk), F32), n_pad=pad)
    zero_state = (jnp.zeros((M_HEADS, dv, dk), F32), jnp.zeros((M_HEADS, 8, dk), F32),
                  jnp.zeros((8, dk), F32))
    _, c_m, n_m, m_m = _mixer(
        jnp.zeros((1, d_s5, TILE_MIX), BF16), q_m, k_m, v_m, o_m, gs_m, gm_m, gcol_m, grow_m,
        *mixer_consts, *zero_state, n_pad=pad, emit_state=True)

    ut, q, k, v, o, gs, gm, gcol, grow, _ = _inproj(x, *inproj_consts, carry_m[0], n_pad=0)
    s5_consts = _s5_weights(s5_lambda_re[0], s5_lambda_im[0], s5_log_dt[0], s5_b_re[0], s5_b_im[0],
                            s5_c_re[0], s5_c_im[0], s5_d[0])
    yst = _s5(ut, ut_m[0, :, TILE_MIX - S5_CHUNK:], *s5_consts)
    (pre,) = _mixer(yst, q, k, v, o, gs, gm, gcol, grow, *mixer_consts, c_m[0], n_m[0], m_m[0],
                    n_pad=0, emit_state=False)
    out = _ffn(x.reshape(bsz * seq, d_model), pre.reshape(bsz * seq, d_model), g0, b0,
               row(ln1_g[0]), row(ln1_b[0]), w_up[0].astype(BF16), row(b_up[0]),
               w_down[0].astype(BF16), row(ln2_g[0]), row(ln2_b[0]), alpha=alpha)
    return out.reshape(bsz, seq, d_model)
```

```python
import functools
import math

import jax
import jax.numpy as jnp
from jax import lax
from jax.experimental import pallas as pl
from jax.experimental.pallas import tpu as pltpu

F32 = jnp.float32
BF16 = jnp.bfloat16

N_META = 16
S5_GROUP = 16
S5_STATE = 64
M_HEADS = 4
CONV_WIDTH = 4
LN_EPS = 1e-5

LANES = 128
V7X_MXU_WIDTH = 256
S5_CHUNK = LANES
TILE_IN = 512
MLSTM_CHUNK = 256
TILE_MIX = 512
TILE_FFN = 512
ROW_BLOCK = 256
CARRY_ROWS = 8
V7X_VMEM_LIMIT = 56 * 1024 * 1024

NEG_INF = float("-inf")


def _layer_norm(x, g, b):
    mu = jnp.mean(x, axis=-1, keepdims=True)
    xc = x - mu
    var = jnp.mean(xc * xc, axis=-1, keepdims=True)
    return xc * lax.rsqrt(var + LN_EPS) * g + b


def _log_sigmoid(x):
    return jnp.minimum(x, 0.0) - jnp.log1p(jnp.exp(-jnp.abs(x)))


def _gelu_tanh(x):
    c = math.sqrt(2.0 / math.pi)
    return x * (0.5 * (1.0 + jnp.tanh(c * (x + 0.044715 * (x * x * x)))))


def _dot(a, b):
    return jnp.dot(a, b, preferred_element_type=F32)


def _dot_nt(a, b):
    return lax.dot_general(a, b, (((1,), (1,)), ((), ())), preferred_element_type=F32)


def _dot_tn(a, b):
    return lax.dot_general(a, b, (((0,), (0,)), ((), ())), preferred_element_type=F32)


def _split3(x):
    hi = x.astype(BF16)
    r = x - hi.astype(F32)
    mid = r.astype(BF16)
    lo = (r - mid.astype(F32)).astype(BF16)
    return hi, mid, lo


def _const_spec(shape):
    nd = len(shape)
    return pl.BlockSpec(shape, lambda *_: (0,) * nd, pipeline_mode=pl.Buffered(1))


def _inproj_kernel(x_ref, g0_ref, b0_ref, wt_ref, bt_ref, w_ref, b_ref, cw_ref, cb_ref,
                   wgc_ref, bgc_ref, wgr_ref, bgr_ref, carry_in_ref,
                   ut_ref, q_ref, k_ref, vt_ref, ot_ref, gst_ref, gmt_ref, gcol_ref, grow_ref,
                   carry_out_ref,
                   ext_ref, *, tile, row_block, n_pad, d_qk, d_s5, d_v, d_model):
    j = pl.program_id(1)

    @pl.when(j == 0)
    def _():
        ext_ref[0:CARRY_ROWS, :] = carry_in_ref[...]

    blocks = [slice(r0, r0 + row_block) for r0 in range(0, tile, row_block)]
    hbs = []
    for rows in blocks:
        hb = _layer_norm(x_ref[rows, :], g0_ref[...], b0_ref[...]).astype(BF16)
        hbs.append(hb)
        qk = _dot(hb, w_ref[...]) + b_ref[...]
        if n_pad:
            t_idx = lax.broadcasted_iota(jnp.int32, (row_block, 1), 0) + (j * tile + rows.start)
            qk = jnp.where(t_idx >= n_pad, qk, 0.0)
        ext_ref[CARRY_ROWS + rows.start:CARRY_ROWS + rows.stop, :] = qk
        gcol_ref[rows, :] = _dot(hb, wgc_ref[...]) + bgc_ref[...]
        grow_ref[:, rows] = _dot_nt(wgr_ref[...], hb) + bgr_ref[...]

    conv = cb_ref[...] + cw_ref[CONV_WIDTH - 1:CONV_WIDTH, :] * ext_ref[pl.ds(CARRY_ROWS, tile), :]
    for r in range(CONV_WIDTH - 1):
        lag = CONV_WIDTH - 1 - r
        conv = conv + cw_ref[r:r + 1, :] * ext_ref[pl.ds(CARRY_ROWS - lag, tile), :]
    tail = ext_ref[pl.ds(tile, CARRY_ROWS), :]
    ext_ref[0:CARRY_ROWS, :] = tail
    carry_out_ref[...] = tail
    act = conv * jax.nn.sigmoid(conv)
    q_ref[...] = act[:, :d_qk].astype(BF16)
    k_ref[...] = (act[:, d_qk:] * ((d_qk // M_HEADS) ** -0.5)).astype(BF16)

    def seg_t(lo, n_rows, rows, hb):
        return _dot_nt(wt_ref[lo:lo + n_rows, :], hb) + bt_ref[lo:lo + n_rows, :]

    for rows, hb in zip(blocks, hbs):
        ut = seg_t(0, d_s5, rows, hb)
        if n_pad:
            cols = lax.broadcasted_iota(jnp.int32, (1, row_block), 1) + (j * tile + rows.start)
            ut = jnp.where(cols >= n_pad, ut, 0.0)
        ut_ref[:, rows] = ut.astype(BF16)
    off = d_s5
    for rows, hb in zip(blocks, hbs):
        vt_ref[:, rows] = seg_t(off, d_v, rows, hb).astype(BF16)
    off += d_v
    for out_ref, n_rows in ((ot_ref, d_v), (gst_ref, d_model), (gmt_ref, d_model)):
        for rows, hb in zip(blocks, hbs):
            out_ref[:, rows] = jax.nn.sigmoid(seg_t(off, n_rows, rows, hb)).astype(BF16)
        off += n_rows


def _inproj(x, g0, b0, w_t, b_t, w_qk, b_qk, conv_w, conv_b, wgc, bgc, wgr, bgr, carry_in,
            *, n_pad):
    bsz, length, d_model = x.shape
    tile = min(TILE_IN, length)
    assert length % tile == 0
    d_qk = conv_w.shape[1] // 2
    d_s5 = d_model // 2
    d_v = d_model
    n_t = w_t.shape[0]
    assert n_t == d_s5 + 2 * d_v + 2 * d_model and w_qk.shape[1] == 2 * d_qk
    grid = (bsz, length // tile)

    def tok(width):
        return pl.BlockSpec((None, tile, width), lambda b, j: (b, j, 0))

    def feat(rows):
        return pl.BlockSpec((None, rows, tile), lambda b, j: (b, 0, j))

    def feat_out(rows):
        return jax.ShapeDtypeStruct((bsz, rows, length), BF16)

    kern = functools.partial(_inproj_kernel, tile=tile, row_block=min(ROW_BLOCK, tile), n_pad=n_pad,
                             d_qk=d_qk, d_s5=d_s5, d_v=d_v, d_model=d_model)
    return pl.pallas_call(
        kern,
        grid=grid,
        in_specs=[tok(d_model), _const_spec((1, d_model)), _const_spec((1, d_model)),
                  _const_spec((n_t, d_model)), _const_spec((n_t, 1)),
                  _const_spec((d_model, 2 * d_qk)), _const_spec((1, 2 * d_qk)),
                  _const_spec((CONV_WIDTH, 2 * d_qk)), _const_spec((1, 2 * d_qk)),
                  _const_spec(wgc.shape), _const_spec(bgc.shape), _const_spec(wgr.shape),
                  _const_spec(bgr.shape), _const_spec((CARRY_ROWS, 2 * d_qk))],
        out_specs=[feat(d_s5), tok(d_qk), tok(d_qk), feat(d_v), feat(d_v), feat(d_model),
                   feat(d_model), tok(LANES), feat(2 * M_HEADS),
                   pl.BlockSpec((None, CARRY_ROWS, 2 * d_qk), lambda b, j: (b, 0, 0))],
        out_shape=[feat_out(d_s5),
                   jax.ShapeDtypeStruct((bsz, length, d_qk), BF16),
                   jax.ShapeDtypeStruct((bsz, length, d_qk), BF16),
                   feat_out(d_v), feat_out(d_v), feat_out(d_model), feat_out(d_model),
                   jax.ShapeDtypeStruct((bsz, length, LANES), F32),
                   jax.ShapeDtypeStruct((bsz, 2 * M_HEADS, length), F32),
                   jax.ShapeDtypeStruct((bsz, CARRY_ROWS, 2 * d_qk), F32)],
        scratch_shapes=[pltpu.VMEM((CARRY_ROWS + tile, 2 * d_qk), F32)],
        compiler_params=pltpu.CompilerParams(
            dimension_semantics=("arbitrary", "arbitrary"), vmem_limit_bytes=V7X_VMEM_LIMIT),
        name="inproj",
    )(x, g0, b0, w_t, b_t, w_qk, b_qk, conv_w, conv_b, wgc, bgc, wgr, bgr, carry_in)


def _s5_kernel(ut_ref, um_ref, taps_ref, wst_ref, wc_ref, a1_ref, a2_ref, dsk_ref, yt_ref,
               px_ref, lhs_ref, toe_ref, s_ref, xin_ref, y_ref, ymid_ref, *, n_chunks, bsz):
    t_len = S5_CHUNK
    n_ch = S5_GROUP
    length = n_chunks * t_len
    half = 2 * S5_STATE
    rows_bj = bsz * n_ch
    lane_step = 1024

    r_idx = lax.broadcasted_iota(jnp.int32, (rows_bj, rows_bj), 0)
    q_idx = lax.broadcasted_iota(jnp.int32, (rows_bj, rows_bj), 1)
    perm = jnp.where(q_idx == (r_idx % bsz) * n_ch + r_idx // bsz, 1.0, 0.0).astype(BF16)
    perm_back = jnp.where(q_idx == (r_idx % n_ch) * bsz + r_idx // n_ch, 1.0, 0.0).astype(BF16)
    x_bj = ut_ref[...].reshape(rows_bj, length)
    for lo in range(0, length, lane_step):
        px_ref[:, lo:lo + lane_step] = _dot(perm, x_bj[:, lo:lo + lane_step]).astype(BF16)

    for jj in range(n_ch):
        lhs_ref[0:bsz, jj * t_len:(jj + 1) * t_len] = jnp.broadcast_to(
            um_ref[jj:jj + 1, :], (bsz, t_len))
    for c in range(n_chunks):
        for jj in range(n_ch):
            lhs_ref[(c + 1) * bsz:(c + 2) * bsz, jj * t_len:(jj + 1) * t_len] = (
                px_ref[jj * bsz:(jj + 1) * bsz, c * t_len:(c + 1) * t_len])

    lhs = lhs_ref[...]
    s_ref[...] = _dot(lhs, wst_ref[...])
    a1 = a1_ref[...]
    a2 = a2_ref[...]
    state = jnp.zeros((bsz, 2 * half), F32)
    for c in range(n_chunks + 1):
        xin_ref[c * bsz:(c + 1) * bsz, :] = state[:, :half]
        swapped = jnp.concatenate([state[:, half:], state[:, :half]], axis=1)
        state = a1 * state + a2 * swapped + s_ref[c * bsz:(c + 1) * bsz, :]
    xin = xin_ref[...].astype(BF16)

    s_io = lax.broadcasted_iota(jnp.int32, (t_len, t_len), 0)
    t_io = lax.broadcasted_iota(jnp.int32, (t_len, t_len), 1)
    lower = t_io >= s_io
    cols_per_step = V7X_MXU_WIDTH // t_len
    for i0 in range(0, n_ch, cols_per_step):
        cols = slice(i0 * t_len, (i0 + cols_per_step) * t_len)
        for ii in range(i0, i0 + cols_per_step):
            for jj in range(n_ch):
                tap = jnp.broadcast_to(taps_ref[jj, ii:ii + 1, :], (t_len, t_len))
                blk = pltpu.roll(tap, 0, 1, stride=1, stride_axis=0)
                toe_ref[jj * t_len:(jj + 1) * t_len, ii * t_len:(ii + 1) * t_len] = (
                    jnp.where(lower, blk, 0.0).astype(BF16))
        y_ref[:, cols] = (_dot(lhs, toe_ref[:, cols]) + _dot(xin, wc_ref[:, cols])
                          + lhs[:, cols].astype(F32) * dsk_ref[:, cols])

    for c in range(n_chunks):
        for ii in range(n_ch):
            ymid_ref[ii * bsz:(ii + 1) * bsz, c * t_len:(c + 1) * t_len] = (
                y_ref[(c + 1) * bsz:(c + 2) * bsz, ii * t_len:(ii + 1) * t_len].astype(BF16))
    for lo in range(0, length, lane_step):
        yt_ref[:, :, lo:lo + lane_step] = _dot(
            perm_back, ymid_ref[:, lo:lo + lane_step]).astype(BF16).reshape(bsz, n_ch, lane_step)


def _s5(ut, um, taps, wst, wc, a1, a2, dskip):
    bsz, d_s5, length = ut.shape
    groups = d_s5 // S5_GROUP
    t_len = S5_CHUNK
    n_chunks = length // t_len
    assert length % 1024 == 0
    width = S5_GROUP * t_len
    half = 2 * S5_STATE
    rows = (n_chunks + 1) * bsz

    def grp(*shape):
        return pl.BlockSpec((None,) + shape, lambda g: (g,) + (0,) * len(shape))

    return pl.pallas_call(
        functools.partial(_s5_kernel, n_chunks=n_chunks, bsz=bsz),
        grid=(groups,),
        in_specs=[pl.BlockSpec((bsz, S5_GROUP, length), lambda g: (0, g, 0)),
                  pl.BlockSpec((S5_GROUP, t_len), lambda g: (g, 0)),
                  grp(S5_GROUP, S5_GROUP, t_len), grp(width, 2 * half), grp(half, width),
                  grp(1, 2 * half), grp(1, 2 * half), grp(1, width)],
        out_specs=pl.BlockSpec((bsz, S5_GROUP, length), lambda g: (0, g, 0)),
        out_shape=jax.ShapeDtypeStruct((bsz, d_s5, length), BF16),
        scratch_shapes=[pltpu.VMEM((bsz * S5_GROUP, length), BF16),
                        pltpu.VMEM((rows, width), BF16),
                        pltpu.VMEM((width, width), BF16),
                        pltpu.VMEM((rows, 2 * half), F32),
                        pltpu.VMEM((rows, half), F32),
                        pltpu.VMEM((rows, width), F32),
                        pltpu.VMEM((bsz * S5_GROUP, length), BF16)],
        compiler_params=pltpu.CompilerParams(
            dimension_semantics=("arbitrary",), vmem_limit_bytes=V7X_VMEM_LIMIT),
        name="s5",
    )(ut, um, taps, wst, wc, a1, a2, dskip)


def _s5_weights(lam_re, lam_im, log_dt, b_re, b_im, c_re, c_im, d_skip):
    hp = lax.Precision.HIGHEST
    t_len = S5_CHUNK
    groups, n_state = lam_re.shape
    lr, li = lam_re.astype(F32), lam_im.astype(F32)
    dt = jnp.exp(log_dt.astype(F32))[:, None]
    ar, ai = lr * dt, li * dt
    steps = jnp.arange(t_len + 1, dtype=F32)[:, None, None]
    mag = jnp.exp(ar[None] * steps)
    pr, pi = mag * jnp.cos(ai[None] * steps), mag * jnp.sin(ai[None] * steps)
    nr, ni = pr[1] - 1.0, pi[1]
    den = lr * lr + li * li
    fr, fi = (nr * lr + ni * li) / den, (ni * lr - nr * li) / den
    bre, bim = b_re.astype(F32), b_im.astype(F32)
    bbr = fr[..., None] * bre - fi[..., None] * bim
    bbi = fr[..., None] * bim + fi[..., None] * bre
    cre, cim = c_re.astype(F32), c_im.astype(F32)
    cpr = cre[None] * pr[:, :, None, :] - cim[None] * pi[:, :, None, :]
    cpi = cre[None] * pi[:, :, None, :] + cim[None] * pr[:, :, None, :]
    taps = (jnp.einsum("dgip,gpj->gjid", cpr[:t_len], bbr, precision=hp)
            - jnp.einsum("dgip,gpj->gjid", cpi[:t_len], bbi, precision=hp))

    rev = t_len - 1 - jnp.arange(t_len)
    qr, qi = pr[rev][..., None], pi[rev][..., None]
    sr = qr * bbr[None] - qi * bbi[None]
    si = qr * bbi[None] + qi * bbr[None]
    sr = jnp.transpose(sr, (1, 3, 0, 2))
    si = jnp.transpose(si, (1, 3, 0, 2))
    wst = jnp.concatenate([sr, si, si, sr], axis=-1).reshape(
        groups, S5_GROUP * t_len, 4 * n_state)

    wc = jnp.concatenate([jnp.transpose(cpr[1:], (1, 3, 2, 0)),
                          -jnp.transpose(cpi[1:], (1, 3, 2, 0))], axis=1).reshape(
        groups, 2 * n_state, S5_GROUP * t_len)

    a_re, a_im = pr[t_len], pi[t_len]
    a1 = jnp.concatenate([a_re, a_re, a_re, a_re], axis=-1)[:, None, :]
    a2 = jnp.concatenate([-a_im, a_im, a_im, -a_im], axis=-1)[:, None, :]
    dsk = jnp.repeat(d_skip.astype(F32).reshape(groups, 1, S5_GROUP), t_len, axis=-1)
    return taps, wst.astype(BF16), wc.astype(BF16), a1, a2, dsk


def _mixer_kernel(yst_ref, q_ref, k_ref, vt_ref, ot_ref, gst_ref, gmt_ref, gcol_ref, grow_ref,
                  wglut_ref, mngt_ref, wmot_ref, wo_ref, c0_ref, n0_ref, m0_ref,
                  *refs, tile, chunk, n_pad, emit_state, d_model):
    if emit_state:
        out_ref, c_out_ref, n_out_ref, m_out_ref, ct_ref, n_ref, m_ref, hmt_ref = refs
    else:
        out_ref, ct_ref, n_ref, m_ref, hmt_ref = refs
    j = pl.program_id(1)
    dk = q_ref.shape[-1] // M_HEADS
    dv = vt_ref.shape[0] // M_HEADS

    @pl.when(j == 0)
    def _():
        ct_ref[...] = c0_ref[...]
        n_ref[...] = n0_ref[...]
        m_ref[...] = m0_ref[...]

    heads = range(M_HEADS)
    chunks = [slice(c0, c0 + chunk) for c0 in range(0, tile, chunk)]
    ti = lax.broadcasted_iota(jnp.int32, (chunk, chunk), 0)
    tj = lax.broadcasted_iota(jnp.int32, (chunk, chunk), 1)
    tri_lower = jnp.where(ti >= tj, 1.0, 0.0).astype(BF16)
    tri_upper = jnp.where(ti <= tj, 1.0, 0.0).astype(BF16)

    qs = [[q_ref[cs, h * dk:(h + 1) * dk] for h in heads] for cs in chunks]
    ks = [[k_ref[cs, h * dk:(h + 1) * dk] for h in heads] for cs in chunks]
    vts = [[vt_ref[h * dv:(h + 1) * dv, cs] for h in heads] for cs in chunks]
    kq = [[_dot_nt(ks[c][h], qs[c][h]) for h in heads] for c in range(len(chunks))]
    zt = [_dot(wglut_ref[...], _gelu_tanh(yst_ref[:, cs].astype(F32)).astype(BF16))
          for cs in chunks]

    m_st = [m_ref[h:h + 1, 0:1] for h in heads]
    ct_st = [ct_ref[h] for h in heads]
    n_st = [n_ref[h] for h in heads]
    m_sts, cq, nq, brs, r_cols = [], [], [], [], []
    for c, cs in enumerate(chunks):
        g_col = gcol_ref[cs, :]
        g_row = grow_ref[:, cs]
        lane = lax.broadcasted_iota(jnp.int32, g_col.shape, 1)
        sub = lax.broadcasted_iota(jnp.int32, g_row.shape, 0)
        lf_col = jnp.where((lane >= M_HEADS) & (lane < 2 * M_HEADS), _log_sigmoid(g_col), 0.0)
        lf_row = jnp.where(sub >= M_HEADS, _log_sigmoid(g_row), 0.0)
        li_col, li_row = g_col, g_row
        if n_pad:
            t_col = lax.broadcasted_iota(jnp.int32, g_col.shape, 0) + (j * tile + cs.start)
            t_row = lax.broadcasted_iota(jnp.int32, g_row.shape, 1) + (j * tile + cs.start)
            lf_col = jnp.where(t_col >= n_pad, lf_col, 0.0)
            lf_row = jnp.where(t_row >= n_pad, lf_row, 0.0)
            li_col = jnp.where(t_col >= n_pad, li_col, NEG_INF)
            li_row = jnp.where(t_row >= n_pad, li_row, NEG_INF)
        b_col = sum(_dot(tri_lower, part) for part in _split3(lf_col))
        b_row = sum(_dot(part, tri_upper) for part in _split3(lf_row))

        brs.append([b_row[M_HEADS + h:M_HEADS + h + 1, :] for h in heads])
        r_cols.append([li_col[:, h:h + 1] - b_col[:, M_HEADS + h:M_HEADS + h + 1] for h in heads])
        r_rows = [li_row[h:h + 1, :] - brs[c][h] for h in heads]

        m_sts.append(m_st)
        cq.append([_dot_nt(ct_st[h].astype(BF16), qs[c][h]) for h in heads])
        nq.append([_dot_nt(n_st[h].astype(BF16), qs[c][h])[0:1, :] for h in heads])
        a_last = [jnp.maximum(m_st[h], jnp.max(r_rows[h], axis=-1, keepdims=True)) for h in heads]
        w_k = [jnp.exp(r_rows[h] - a_last[h]) for h in heads]
        upd = [_dot((vts[c][h].astype(F32) * w_k[h]).astype(BF16), ks[c][h]) for h in heads]
        n_upd = [_dot(jnp.broadcast_to(w_k[h], (8, chunk)).astype(BF16), ks[c][h]) for h in heads]
        decay = [jnp.exp(m_st[h] - a_last[h]) for h in heads]
        ct_st = [decay[h] * ct_st[h] + upd[h] for h in heads]
        n_st = [decay[h] * n_st[h] + n_upd[h] for h in heads]
        m_st = [brs[c][h][:, chunk - 1:chunk] + a_last[h] for h in heads]
    for h in heads:
        ct_ref[h] = ct_st[h]
        n_ref[h] = n_st[h]
        m_ref[h:h + 1, :] = jnp.broadcast_to(m_st[h], (1, m_ref.shape[1]))

    a_rows, w_inters, s_mats, vs_t = [], [], [], []
    for c in range(len(chunks)):
        e_mats = [jnp.where(ti <= tj, r_cols[c][h], NEG_INF) for h in heads]
        a_rows.append([jnp.maximum(m_sts[c][h], jnp.max(e_mats[h], axis=0, keepdims=True))
                       for h in heads])
        w_inters.append([jnp.exp(m_sts[c][h] - a_rows[c][h]) for h in heads])
        s_mats.append([kq[c][h] * jnp.exp(e_mats[h] - a_rows[c][h]) for h in heads])
        vs_t.append([_dot(vts[c][h], s_mats[c][h].astype(BF16)) for h in heads])

    y_m = []
    for c, cs in enumerate(chunks):
        for h in heads:
            den = jnp.sum(s_mats[c][h], axis=0, keepdims=True) + w_inters[c][h] * nq[c][h]
            scale = 1.0 / jnp.maximum(jnp.abs(den), jnp.exp(-(brs[c][h] + a_rows[c][h])))
            hh = (vs_t[c][h] + w_inters[c][h] * cq[c][h]) * scale
            hc = hh - jnp.mean(hh, axis=0, keepdims=True)
            var = jnp.mean(hc * hc, axis=0, keepdims=True)
            hn = hc * lax.rsqrt(var + LN_EPS) * mngt_ref[h * dv:(h + 1) * dv, :]
            hmt_ref[h * dv:(h + 1) * dv, cs] = (
                ot_ref[h * dv:(h + 1) * dv, cs].astype(F32) * hn).astype(BF16)
        y_m.append(_dot(wmot_ref[...], hmt_ref[:, cs]))

    for c, cs in enumerate(chunks):
        y_s5 = zt[c][:d_model, :] * jax.nn.sigmoid(zt[c][d_model:, :])
        mix = gst_ref[:, cs].astype(F32) * y_s5 + gmt_ref[:, cs].astype(F32) * y_m[c]
        out_ref[cs, :] = _dot_tn(mix.astype(BF16), wo_ref[...])
    if emit_state:
        c_out_ref[...] = ct_ref[...]
        n_out_ref[...] = n_ref[...]
        m_out_ref[...] = m_ref[...]


def _mixer(yst, q, k, vt, ot, gst, gmt, gcol, grow, wglut, mngt, wmot, wo, c0, n0, m0,
           *, n_pad, emit_state):
    bsz, d_model, length = gst.shape
    tile = min(TILE_MIX, length)
    chunk = MLSTM_CHUNK
    assert length % tile == 0 and tile % chunk == 0 and mngt.shape[1] == chunk
    d_qk = q.shape[-1]
    d_v = vt.shape[1]
    d_s5 = yst.shape[1]
    dk, dv = d_qk // M_HEADS, d_v // M_HEADS
    grid = (bsz, length // tile)

    def tok(width):
        return pl.BlockSpec((None, tile, width), lambda b, j: (b, j, 0))

    def feat(rows):
        return pl.BlockSpec((None, rows, tile), lambda b, j: (b, 0, j))

    consts = [wglut, mngt, wmot, wo, c0, n0, m0]
    in_specs = ([feat(d_s5), tok(d_qk), tok(d_qk), feat(d_v), feat(d_v), feat(d_model),
                 feat(d_model), tok(LANES), feat(2 * M_HEADS)]
                + [_const_spec(c.shape) for c in consts])
    out_specs = [tok(d_model)]
    out_shape = [jax.ShapeDtypeStruct((bsz, length, d_model), F32)]
    if emit_state:
        out_specs += [pl.BlockSpec((None, M_HEADS, dv, dk), lambda b, j: (b, 0, 0, 0)),
                      pl.BlockSpec((None, M_HEADS, 8, dk), lambda b, j: (b, 0, 0, 0)),
                      pl.BlockSpec((None, 8, dk), lambda b, j: (b, 0, 0))]
        out_shape += [jax.ShapeDtypeStruct((bsz, M_HEADS, dv, dk), F32),
                      jax.ShapeDtypeStruct((bsz, M_HEADS, 8, dk), F32),
                      jax.ShapeDtypeStruct((bsz, 8, dk), F32)]
    kern = functools.partial(_mixer_kernel, tile=tile, chunk=chunk, n_pad=n_pad,
                             emit_state=emit_state, d_model=d_model)
    return pl.pallas_call(
        kern,
        grid=grid,
        in_specs=in_specs,
        out_specs=out_specs,
        out_shape=out_shape,
        scratch_shapes=[pltpu.VMEM((M_HEADS, dv, dk), F32), pltpu.VMEM((M_HEADS, 8, dk), F32),
                        pltpu.VMEM((8, dk), F32), pltpu.VMEM((d_v, tile), BF16)],
        compiler_params=pltpu.CompilerParams(
            dimension_semantics=("arbitrary", "arbitrary"), vmem_limit_bytes=V7X_VMEM_LIMIT),
        name="mixer",
    )(yst, q, k, vt, ot, gst, gmt, gcol, grow, *consts)


def _ffn_kernel(x_ref, pre_ref, g0_ref, b0_ref, g1_ref, b1_ref, wu_ref, bu_ref, wd_ref, g2_ref,
                b2_ref, out_ref, *, alpha, ff_chunk, row_block):
    d_ff = wu_ref.shape[1]
    tile = x_ref.shape[0]
    for r0 in range(0, tile, row_block):
        rows = slice(r0, r0 + row_block)
        h0 = _layer_norm(x_ref[rows, :], g0_ref[...], b0_ref[...])
        h = _layer_norm(alpha * h0 + pre_ref[rows, :], g1_ref[...], b1_ref[...])
        hb = h.astype(BF16)
        acc = alpha * h
        for c in range(d_ff // ff_chunk):
            lo = c * ff_chunk
            a = jnp.maximum(
                _dot(hb, wu_ref[:, lo:lo + ff_chunk]) + bu_ref[:, lo:lo + ff_chunk], 0.0)
            acc = acc + _dot((a * a).astype(BF16), wd_ref[lo:lo + ff_chunk, :])
        out_ref[rows, :] = _layer_norm(acc, g2_ref[...], b2_ref[...])


def _ffn(x, pre, g0, b0, g1, b1, wu, bu, wd, g2, b2, *, alpha):
    rows, d_model = x.shape
    tile = TILE_FFN
    assert rows % tile == 0
    d_ff = wu.shape[1]
    return pl.pallas_call(
        functools.partial(_ffn_kernel, alpha=alpha, ff_chunk=min(1024, d_ff),
                          row_block=min(ROW_BLOCK, tile)),
        grid=(rows // tile,),
        in_specs=[pl.BlockSpec((tile, d_model), lambda i: (i, 0)),
                  pl.BlockSpec((tile, d_model), lambda i: (i, 0)),
                  _const_spec((1, d_model)), _const_spec((1, d_model)),
                  _const_spec((1, d_model)), _const_spec((1, d_model)),
                  _const_spec((d_model, d_ff)), _const_spec((1, d_ff)),
                  _const_spec((d_ff, d_model)), _const_spec((1, d_model)),
                  _const_spec((1, d_model))],
        out_specs=pl.BlockSpec((tile, d_model), lambda i: (i, 0)),
        out_shape=jax.ShapeDtypeStruct((rows, d_model), F32),
        compiler_params=pltpu.CompilerParams(
            dimension_semantics=("arbitrary",), vmem_limit_bytes=V7X_VMEM_LIMIT),
        name="ffn",
    )(x, pre, g0, b0, g1, b1, wu, bu, wd, g2, b2)


def kernel(x, meta_tokens, ln0_g, ln0_b, w_in, b_in, qk_conv_w, qk_conv_b, s5_lambda_re, s5_lambda_im, s5_log_dt, s5_b_re, s5_b_im, s5_c_re, s5_c_im, s5_d, s5_w_glu, m_norm_g, m_w_out, w_o, ln1_g, ln1_b, w_up, b_up, w_down, ln2_g, ln2_b):
    bsz, seq, d_model = x.shape
    depth = w_in.shape[0]
    assert depth == 1, "the meta-token prologue is written for a single layer"
    assert meta_tokens.shape == (N_META, d_model)
    assert seq % TILE_MIX == 0 and seq % S5_CHUNK == 0 and MLSTM_CHUNK >= S5_CHUNK >= N_META
    alpha = (2.0 * depth) ** 0.25
    d_s5 = d_model // 2
    d_qk = qk_conv_w.shape[-1] // 2
    d_v = m_norm_g.shape[-1]
    dk, dv = d_qk // M_HEADS, d_v // M_HEADS
    gate_lo = d_s5 + 2 * d_qk + 2 * d_v
    gate_hi = gate_lo + 2 * M_HEADS
    assert w_in.shape[-1] == gate_hi + 2 * d_model

    row = lambda a: a.reshape(1, -1).astype(F32)
    w = w_in[0]
    bias = b_in[0]
    qk_lo, qk_hi = d_s5, d_s5 + 2 * d_qk
    w_t = jnp.concatenate([w[:, :qk_lo], w[:, qk_hi:gate_lo], w[:, gate_hi:]], axis=1).T.astype(BF16)
    b_t = jnp.concatenate([bias[:qk_lo], bias[qk_hi:gate_lo], bias[gate_hi:]]).reshape(-1, 1).astype(F32)
    w_qk = w[:, qk_lo:qk_hi].astype(BF16)
    b_qk = row(bias[qk_lo:qk_hi])
    w_gate = w[:, gate_lo:gate_hi]
    wgc = jnp.pad(w_gate, ((0, 0), (0, LANES - 2 * M_HEADS))).astype(BF16)
    bgc = jnp.pad(bias[gate_lo:gate_hi], (0, LANES - 2 * M_HEADS)).reshape(1, LANES).astype(F32)
    wgr = w_gate.T.astype(BF16)
    bgr = bias[gate_lo:gate_hi].reshape(2 * M_HEADS, 1).astype(F32)
    g0, b0 = row(ln0_g), row(ln0_b)
    conv_w = qk_conv_w[0].astype(F32)
    conv_b = row(qk_conv_b[0])
    mngt = jnp.broadcast_to(m_norm_g[0].astype(F32)[:, None], (d_v, MLSTM_CHUNK))
    mixer_consts = (s5_w_glu[0].T.astype(BF16), mngt, m_w_out[0].T.astype(BF16),
                    w_o[0].astype(BF16))
    inproj_consts = (g0, b0, w_t, b_t, w_qk, b_qk, conv_w, conv_b, wgc, bgc, wgr, bgr)

    pad = MLSTM_CHUNK - N_META
    x_meta = jnp.concatenate([jnp.zeros((pad, d_model), x.dtype), meta_tokens.astype(x.dtype)])[None]
    ut_m, q_m, k_m, v_m, o_m, gs_m, gm_m, gcol_m, grow_m, carry_m = _inproj(
        x_meta, *inproj_consts, jnp.zeros((CARRY_ROWS, 2 * d_qk), F32), n_pad=pad)
    zero_state = (jnp.zeros((M_HEADS, dv, dk), F32), jnp.zeros((M_HEADS, 8, dk), F32),
                  jnp.zeros((8, dk), F32))
    _, c_m, n_m, m_m = _mixer(
        jnp.zeros((1, d_s5, MLSTM_CHUNK), BF16), q_m, k_m, v_m, o_m, gs_m, gm_m, gcol_m, grow_m,
        *mixer_consts, *zero_state, n_pad=pad, emit_state=True)

    ut, q, k, v, o, gs, gm, gcol, grow, _ = _inproj(x, *inproj_consts, carry_m[0], n_pad=0)
    s5_consts = _s5_weights(s5_lambda_re[0], s5_lambda_im[0], s5_log_dt[0], s5_b_re[0], s5_b_im[0],
                            s5_c_re[0], s5_c_im[0], s5_d[0])
    yst = _s5(ut, ut_m[0, :, MLSTM_CHUNK - S5_CHUNK:], *s5_consts)
    (pre,) = _mixer(yst, q, k, v, o, gs, gm, gcol, grow, *mixer_consts, c_m[0], n_m[0], m_m[0],
                    n_pad=0, emit_state=False)
    out = _ffn(x.reshape(bsz * seq, d_model), pre.reshape(bsz * seq, d_model), g0, b0,
               row(ln1_g[0]), row(ln1_b[0]), w_up[0].astype(BF16), row(b_up[0]),
               w_down[0].astype(BF16), row(ln2_g[0]), row(ln2_b[0]), alpha=alpha)
    return out.reshape(bsz, seq, d_model)
```

```python
import functools
import math

import jax
import jax.numpy as jnp
from jax import lax
from jax.experimental import pallas as pl
from jax.experimental.pallas import tpu as pltpu

F32 = jnp.float32
BF16 = jnp.bfloat16

N_META = 16
S5_GROUP = 16
S5_STATE = 64
M_HEADS = 4
CONV_WIDTH = 4
LN_EPS = 1e-5

LANES = 128
V7X_MXU_WIDTH = 256
S5_CHUNK = LANES
TILE_IN = 512
MLSTM_CHUNK = 256
TILE_MIX = 512
TILE_FFN = 512
ROW_BLOCK = 256
CARRY_ROWS = 8
V7X_VMEM_LIMIT = 56 * 1024 * 1024

NEG_INF = float("-inf")


def _layer_norm(x, g, b):
    mu = jnp.mean(x, axis=-1, keepdims=True)
    xc = x - mu
    var = jnp.mean(xc * xc, axis=-1, keepdims=True)
    return xc * lax.rsqrt(var + LN_EPS) * g + b


def _sigmoid(x):
    return 0.5 * jnp.tanh(0.5 * x) + 0.5


def _log_sigmoid(x):
    return jnp.minimum(x, 0.0) - jnp.log1p(jnp.exp(-jnp.abs(x)))


def _gelu_tanh(x):
    c = math.sqrt(2.0 / math.pi)
    return x * (0.5 * (1.0 + jnp.tanh(c * (x + 0.044715 * (x * x * x)))))


def _dot(a, b):
    return jnp.dot(a, b, preferred_element_type=F32)


def _dot_nt(a, b):
    return lax.dot_general(a, b, (((1,), (1,)), ((), ())), preferred_element_type=F32)


def _dot_tn(a, b):
    return lax.dot_general(a, b, (((0,), (0,)), ((), ())), preferred_element_type=F32)


def _split3(x):
    hi = x.astype(BF16)
    r = x - hi.astype(F32)
    mid = r.astype(BF16)
    lo = (r - mid.astype(F32)).astype(BF16)
    return hi, mid, lo


def _const_spec(shape):
    nd = len(shape)
    return pl.BlockSpec(shape, lambda *_: (0,) * nd, pipeline_mode=pl.Buffered(1))


def _inproj_kernel(x_ref, g0_ref, b0_ref, wt_ref, bt_ref, w_ref, b_ref, cw_ref, cb_ref,
                   wgc_ref, bgc_ref, wgr_ref, bgr_ref, carry_in_ref,
                   ut_ref, q_ref, k_ref, vt_ref, ot_ref, gst_ref, gmt_ref, gcol_ref, grow_ref,
                   carry_out_ref,
                   ext_ref, *, tile, row_block, n_pad, d_qk, d_s5, d_v, d_model):
    j = pl.program_id(1)

    @pl.when(j == 0)
    def _():
        ext_ref[0:CARRY_ROWS, :] = carry_in_ref[...]

    blocks = [slice(r0, r0 + row_block) for r0 in range(0, tile, row_block)]
    hbs = []
    for rows in blocks:
        hb = _layer_norm(x_ref[rows, :], g0_ref[...], b0_ref[...]).astype(BF16)
        hbs.append(hb)
        qk = _dot(hb, w_ref[...]) + b_ref[...]
        if n_pad:
            t_idx = lax.broadcasted_iota(jnp.int32, (row_block, 1), 0) + (j * tile + rows.start)
            qk = jnp.where(t_idx >= n_pad, qk, 0.0)
        ext_ref[CARRY_ROWS + rows.start:CARRY_ROWS + rows.stop, :] = qk
        gcol_ref[rows, :] = _dot(hb, wgc_ref[...]) + bgc_ref[...]
        grow_ref[:, rows] = _dot_nt(wgr_ref[...], hb) + bgr_ref[...]

    def seg_t(lo, n_rows, rows, hb):
        return _dot_nt(wt_ref[lo:lo + n_rows, :], hb) + bt_ref[lo:lo + n_rows, :]

    for rows, hb in zip(blocks, hbs):
        ut = seg_t(0, d_s5, rows, hb)
        if n_pad:
            cols = lax.broadcasted_iota(jnp.int32, (1, row_block), 1) + (j * tile + rows.start)
            ut = jnp.where(cols >= n_pad, ut, 0.0)
        ut_ref[:, rows] = ut.astype(BF16)
    off = d_s5
    for rows, hb in zip(blocks, hbs):
        vt_ref[:, rows] = seg_t(off, d_v, rows, hb).astype(BF16)
    off += d_v

    ext = ext_ref[...]
    conv = cb_ref[...] + cw_ref[CONV_WIDTH - 1:CONV_WIDTH, :] * ext[CARRY_ROWS:, :]
    for r in range(CONV_WIDTH - 1):
        lag = CONV_WIDTH - 1 - r
        conv = conv + cw_ref[r:r + 1, :] * pltpu.roll(ext, lag, 0)[CARRY_ROWS:, :]
    tail = ext[tile:, :]
    ext_ref[0:CARRY_ROWS, :] = tail
    carry_out_ref[...] = tail
    act = conv * _sigmoid(conv)
    q_ref[...] = act[:, :d_qk].astype(BF16)
    k_ref[...] = (act[:, d_qk:] * ((d_qk // M_HEADS) ** -0.5)).astype(BF16)

    for out_ref, n_rows in ((ot_ref, d_v), (gst_ref, d_model), (gmt_ref, d_model)):
        for rows, hb in zip(blocks, hbs):
            out_ref[:, rows] = _sigmoid(seg_t(off, n_rows, rows, hb)).astype(BF16)
        off += n_rows


def _inproj(x, g0, b0, w_t, b_t, w_qk, b_qk, conv_w, conv_b, wgc, bgc, wgr, bgr, carry_in,
            *, n_pad):
    bsz, length, d_model = x.shape
    tile = min(TILE_IN, length)
    assert length % tile == 0
    d_qk = conv_w.shape[1] // 2
    d_s5 = d_model // 2
    d_v = d_model
    n_t = w_t.shape[0]
    assert n_t == d_s5 + 2 * d_v + 2 * d_model and w_qk.shape[1] == 2 * d_qk
    grid = (bsz, length // tile)

    def tok(width):
        return pl.BlockSpec((None, tile, width), lambda b, j: (b, j, 0))

    def feat(rows):
        return pl.BlockSpec((None, rows, tile), lambda b, j: (b, 0, j))

    def feat_out(rows):
        return jax.ShapeDtypeStruct((bsz, rows, length), BF16)

    kern = functools.partial(_inproj_kernel, tile=tile, row_block=min(ROW_BLOCK, tile), n_pad=n_pad,
                             d_qk=d_qk, d_s5=d_s5, d_v=d_v, d_model=d_model)
    return pl.pallas_call(
        kern,
        grid=grid,
        in_specs=[tok(d_model), _const_spec((1, d_model)), _const_spec((1, d_model)),
                  _const_spec((n_t, d_model)), _const_spec((n_t, 1)),
                  _const_spec((d_model, 2 * d_qk)), _const_spec((1, 2 * d_qk)),
                  _const_spec((CONV_WIDTH, 2 * d_qk)), _const_spec((1, 2 * d_qk)),
                  _const_spec(wgc.shape), _const_spec(bgc.shape), _const_spec(wgr.shape),
                  _const_spec(bgr.shape), _const_spec((CARRY_ROWS, 2 * d_qk))],
        out_specs=[feat(d_s5), tok(d_qk), tok(d_qk), feat(d_v), feat(d_v), feat(d_model),
                   feat(d_model), tok(LANES), feat(2 * M_HEADS),
                   pl.BlockSpec((None, CARRY_ROWS, 2 * d_qk), lambda b, j: (b, 0, 0))],
        out_shape=[feat_out(d_s5),
                   jax.ShapeDtypeStruct((bsz, length, d_qk), BF16),
                   jax.ShapeDtypeStruct((bsz, length, d_qk), BF16),
                   feat_out(d_v), feat_out(d_v), feat_out(d_model), feat_out(d_model),
                   jax.ShapeDtypeStruct((bsz, length, LANES), F32),
                   jax.ShapeDtypeStruct((bsz, 2 * M_HEADS, length), F32),
                   jax.ShapeDtypeStruct((bsz, CARRY_ROWS, 2 * d_qk), F32)],
        scratch_shapes=[pltpu.VMEM((CARRY_ROWS + tile, 2 * d_qk), F32)],
        compiler_params=pltpu.CompilerParams(
            dimension_semantics=("arbitrary", "arbitrary"), vmem_limit_bytes=V7X_VMEM_LIMIT),
        name="inproj",
    )(x, g0, b0, w_t, b_t, w_qk, b_qk, conv_w, conv_b, wgc, bgc, wgr, bgr, carry_in)


def _s5_kernel(ut_ref, um_ref, taps_ref, wst_ref, wc_ref, a1_ref, a2_ref, dsk_ref, yt_ref,
               px_ref, lhs_ref, toe_ref, s_ref, xin_ref, y_ref, ymid_ref, *, n_chunks, bsz):
    t_len = S5_CHUNK
    n_ch = S5_GROUP
    length = n_chunks * t_len
    half = 2 * S5_STATE
    rows_bj = bsz * n_ch
    lane_step = 1024

    r_idx = lax.broadcasted_iota(jnp.int32, (rows_bj, rows_bj), 0)
    q_idx = lax.broadcasted_iota(jnp.int32, (rows_bj, rows_bj), 1)
    perm = jnp.where(q_idx == (r_idx % bsz) * n_ch + r_idx // bsz, 1.0, 0.0).astype(BF16)
    perm_back = jnp.where(q_idx == (r_idx % n_ch) * bsz + r_idx // n_ch, 1.0, 0.0).astype(BF16)
    x_bj = ut_ref[...].reshape(rows_bj, length)
    for lo in range(0, length, lane_step):
        px_ref[:, lo:lo + lane_step] = _dot(perm, x_bj[:, lo:lo + lane_step]).astype(BF16)

    for jj in range(n_ch):
        lhs_ref[0:bsz, jj * t_len:(jj + 1) * t_len] = jnp.broadcast_to(
            um_ref[jj:jj + 1, :], (bsz, t_len))
    for c in range(n_chunks):
        for jj in range(n_ch):
            lhs_ref[(c + 1) * bsz:(c + 2) * bsz, jj * t_len:(jj + 1) * t_len] = (
                px_ref[jj * bsz:(jj + 1) * bsz, c * t_len:(c + 1) * t_len])

    lhs = lhs_ref[...]
    s_ref[...] = _dot(lhs, wst_ref[...])
    a1 = a1_ref[...]
    a2 = a2_ref[...]
    state = jnp.zeros((bsz, 2 * half), F32)
    for c in range(n_chunks + 1):
        xin_ref[c * bsz:(c + 1) * bsz, :] = state[:, :half]
        swapped = jnp.concatenate([state[:, half:], state[:, :half]], axis=1)
        state = a1 * state + a2 * swapped + s_ref[c * bsz:(c + 1) * bsz, :]
    xin = xin_ref[...].astype(BF16)

    s_io = lax.broadcasted_iota(jnp.int32, (t_len, t_len), 0)
    t_io = lax.broadcasted_iota(jnp.int32, (t_len, t_len), 1)
    lower = t_io >= s_io
    cols_per_step = V7X_MXU_WIDTH // t_len
    for i0 in range(0, n_ch, cols_per_step):
        cols = slice(i0 * t_len, (i0 + cols_per_step) * t_len)
        for ii in range(i0, i0 + cols_per_step):
            for jj in range(n_ch):
                tap = jnp.broadcast_to(taps_ref[jj, ii:ii + 1, :], (t_len, t_len))
                blk = pltpu.roll(tap, 0, 1, stride=1, stride_axis=0)
                toe_ref[jj * t_len:(jj + 1) * t_len, ii * t_len:(ii + 1) * t_len] = (
                    jnp.where(lower, blk, 0.0).astype(BF16))
        y_ref[:, cols] = (_dot(lhs, toe_ref[:, cols]) + _dot(xin, wc_ref[:, cols])
                          + lhs[:, cols].astype(F32) * dsk_ref[:, cols])

    for c in range(n_chunks):
        for ii in range(n_ch):
            ymid_ref[ii * bsz:(ii + 1) * bsz, c * t_len:(c + 1) * t_len] = (
                y_ref[(c + 1) * bsz:(c + 2) * bsz, ii * t_len:(ii + 1) * t_len].astype(BF16))
    for lo in range(0, length, lane_step):
        yt_ref[:, :, lo:lo + lane_step] = _dot(
            perm_back, ymid_ref[:, lo:lo + lane_step]).astype(BF16).reshape(bsz, n_ch, lane_step)


def _s5(ut, um, taps, wst, wc, a1, a2, dskip):
    bsz, d_s5, length = ut.shape
    groups = d_s5 // S5_GROUP
    t_len = S5_CHUNK
    n_chunks = length // t_len
    assert length % 1024 == 0
    width = S5_GROUP * t_len
    half = 2 * S5_STATE
    rows = (n_chunks + 1) * bsz

    def grp(*shape):
        return pl.BlockSpec((None,) + shape, lambda g: (g,) + (0,) * len(shape))

    return pl.pallas_call(
        functools.partial(_s5_kernel, n_chunks=n_chunks, bsz=bsz),
        grid=(groups,),
        in_specs=[pl.BlockSpec((bsz, S5_GROUP, length), lambda g: (0, g, 0)),
                  pl.BlockSpec((S5_GROUP, t_len), lambda g: (g, 0)),
                  grp(S5_GROUP, S5_GROUP, t_len), grp(width, 2 * half), grp(half, width),
                  grp(1, 2 * half), grp(1, 2 * half), grp(1, width)],
        out_specs=pl.BlockSpec((bsz, S5_GROUP, length), lambda g: (0, g, 0)),
        out_shape=jax.ShapeDtypeStruct((bsz, d_s5, length), BF16),
        scratch_shapes=[pltpu.VMEM((bsz * S5_GROUP, length), BF16),
                        pltpu.VMEM((rows, width), BF16),
                        pltpu.VMEM((width, width), BF16),
                        pltpu.VMEM((rows, 2 * half), F32),
                        pltpu.VMEM((rows, half), F32),
                        pltpu.VMEM((rows, width), F32),
                        pltpu.VMEM((bsz * S5_GROUP, length), BF16)],
        compiler_params=pltpu.CompilerParams(
            dimension_semantics=("arbitrary",), vmem_limit_bytes=V7X_VMEM_LIMIT),
        name="s5",
    )(ut, um, taps, wst, wc, a1, a2, dskip)


def _s5_weights(lam_re, lam_im, log_dt, b_re, b_im, c_re, c_im, d_skip):
    hp = lax.Precision.HIGHEST
    t_len = S5_CHUNK
    groups, n_state = lam_re.shape
    lr, li = lam_re.astype(F32), lam_im.astype(F32)
    dt = jnp.exp(log_dt.astype(F32))[:, None]
    ar, ai = lr * dt, li * dt
    steps = jnp.arange(t_len + 1, dtype=F32)[:, None, None]
    mag = jnp.exp(ar[None] * steps)
    pr, pi = mag * jnp.cos(ai[None] * steps), mag * jnp.sin(ai[None] * steps)
    nr, ni = pr[1] - 1.0, pi[1]
    den = lr * lr + li * li
    fr, fi = (nr * lr + ni * li) / den, (ni * lr - nr * li) / den
    bre, bim = b_re.astype(F32), b_im.astype(F32)
    bbr = fr[..., None] * bre - fi[..., None] * bim
    bbi = fr[..., None] * bim + fi[..., None] * bre
    cre, cim = c_re.astype(F32), c_im.astype(F32)
    cbr = cre[:, :, None, :] * jnp.transpose(bbr, (0, 2, 1))[:, None] \
        - cim[:, :, None, :] * jnp.transpose(bbi, (0, 2, 1))[:, None]
    cbi = cre[:, :, None, :] * jnp.transpose(bbi, (0, 2, 1))[:, None] \
        + cim[:, :, None, :] * jnp.transpose(bbr, (0, 2, 1))[:, None]
    taps = (jnp.einsum("gijp,dgp->gjid", cbr, pr[:t_len], precision=hp)
            - jnp.einsum("gijp,dgp->gjid", cbi, pi[:t_len], precision=hp))

    prg = jnp.transpose(pr[:t_len][::-1], (1, 0, 2))[:, None]
    pig = jnp.transpose(pi[:t_len][::-1], (1, 0, 2))[:, None]
    bjr = jnp.transpose(bbr, (0, 2, 1))[:, :, None, :]
    bji = jnp.transpose(bbi, (0, 2, 1))[:, :, None, :]
    sr = (prg * bjr - pig * bji).astype(BF16)
    si = (prg * bji + pig * bjr).astype(BF16)
    wst = jnp.concatenate([sr, si, si, sr], axis=-1).reshape(
        groups, S5_GROUP * t_len, 4 * n_state)

    ptr = jnp.transpose(pr[1:], (1, 2, 0))[:, :, None, :]
    pti = jnp.transpose(pi[1:], (1, 2, 0))[:, :, None, :]
    cpr = jnp.transpose(cre, (0, 2, 1))[..., None]
    cpi = jnp.transpose(cim, (0, 2, 1))[..., None]
    wc = jnp.concatenate([(cpr * ptr - cpi * pti).astype(BF16),
                          (-(cpr * pti + cpi * ptr)).astype(BF16)], axis=1).reshape(
        groups, 2 * n_state, S5_GROUP * t_len)

    a_re, a_im = pr[t_len], pi[t_len]
    a1 = jnp.concatenate([a_re, a_re, a_re, a_re], axis=-1)[:, None, :]
    a2 = jnp.concatenate([-a_im, a_im, a_im, -a_im], axis=-1)[:, None, :]
    dsk = jnp.repeat(d_skip.astype(F32).reshape(groups, 1, S5_GROUP), t_len, axis=-1)
    return taps, wst, wc, a1, a2, dsk


def _mixer_kernel(yst_ref, q_ref, k_ref, vt_ref, ot_ref, gst_ref, gmt_ref, gcol_ref, grow_ref,
                  wglut_ref, mngt_ref, wmot_ref, wo_ref, c0_ref, n0_ref, m0_ref,
                  *refs, tile, chunk, n_pad, emit_state, d_model):
    if emit_state:
        out_ref, c_out_ref, n_out_ref, m_out_ref, ct_ref, n_ref, m_ref, hmt_ref = refs
    else:
        out_ref, ct_ref, n_ref, m_ref, hmt_ref = refs
    j = pl.program_id(1)
    dk = q_ref.shape[-1] // M_HEADS
    dv = vt_ref.shape[0] // M_HEADS

    @pl.when(j == 0)
    def _():
        ct_ref[...] = c0_ref[...]
        n_ref[...] = n0_ref[...]
        m_ref[...] = m0_ref[...]

    heads = range(M_HEADS)
    chunks = [slice(c0, c0 + chunk) for c0 in range(0, tile, chunk)]
    ti = lax.broadcasted_iota(jnp.int32, (chunk, chunk), 0)
    tj = lax.broadcasted_iota(jnp.int32, (chunk, chunk), 1)
    tri_lower = jnp.where(ti >= tj, 1.0, 0.0).astype(BF16)
    tri_upper = jnp.where(ti <= tj, 1.0, 0.0).astype(BF16)

    qs = [[q_ref[cs, h * dk:(h + 1) * dk] for h in heads] for cs in chunks]
    ks = [[k_ref[cs, h * dk:(h + 1) * dk] for h in heads] for cs in chunks]
    vts = [[vt_ref[h * dv:(h + 1) * dv, cs] for h in heads] for cs in chunks]
    kq = [[_dot_nt(ks[c][h], qs[c][h]) for h in heads] for c in range(len(chunks))]
    zt = [_dot(wglut_ref[...], _gelu_tanh(yst_ref[:, cs].astype(F32)).astype(BF16))
          for cs in chunks]

    m_st = [m_ref[h:h + 1, 0:1] for h in heads]
    ct_st = [ct_ref[h] for h in heads]
    n_st = [n_ref[h] for h in heads]
    m_sts, cq, nq, brs, r_cols = [], [], [], [], []
    for c, cs in enumerate(chunks):
        g_col = gcol_ref[cs, :]
        g_row = grow_ref[:, cs]
        lane = lax.broadcasted_iota(jnp.int32, g_col.shape, 1)
        sub = lax.broadcasted_iota(jnp.int32, g_row.shape, 0)
        lf_col = jnp.where((lane >= M_HEADS) & (lane < 2 * M_HEADS), _log_sigmoid(g_col), 0.0)
        lf_row = jnp.where(sub >= M_HEADS, _log_sigmoid(g_row), 0.0)
        li_col, li_row = g_col, g_row
        if n_pad:
            t_col = lax.broadcasted_iota(jnp.int32, g_col.shape, 0) + (j * tile + cs.start)
            t_row = lax.broadcasted_iota(jnp.int32, g_row.shape, 1) + (j * tile + cs.start)
            lf_col = jnp.where(t_col >= n_pad, lf_col, 0.0)
            lf_row = jnp.where(t_row >= n_pad, lf_row, 0.0)
            li_col = jnp.where(t_col >= n_pad, li_col, NEG_INF)
            li_row = jnp.where(t_row >= n_pad, li_row, NEG_INF)
        b_col = sum(_dot(tri_lower, part) for part in _split3(lf_col))
        b_row = sum(_dot(part, tri_upper) for part in _split3(lf_row))

        brs.append([b_row[M_HEADS + h:M_HEADS + h + 1, :] for h in heads])
        r_cols.append([li_col[:, h:h + 1] - b_col[:, M_HEADS + h:M_HEADS + h + 1] for h in heads])
        r_rows = [li_row[h:h + 1, :] - brs[c][h] for h in heads]

        m_sts.append(m_st)
        cq.append([_dot_nt(ct_st[h].astype(BF16), qs[c][h]) for h in heads])
        nq.append([_dot_nt(n_st[h].astype(BF16), qs[c][h])[0:1, :] for h in heads])
        a_last = [jnp.maximum(m_st[h], jnp.max(r_rows[h], axis=-1, keepdims=True)) for h in heads]
        w_k = [jnp.exp(r_rows[h] - a_last[h]) for h in heads]
        upd = [_dot((vts[c][h].astype(F32) * w_k[h]).astype(BF16), ks[c][h]) for h in heads]
        n_upd = [_dot(jnp.broadcast_to(w_k[h], (8, chunk)).astype(BF16), ks[c][h]) for h in heads]
        decay = [jnp.exp(m_st[h] - a_last[h]) for h in heads]
        ct_st = [decay[h] * ct_st[h] + upd[h] for h in heads]
        n_st = [decay[h] * n_st[h] + n_upd[h] for h in heads]
        m_st = [brs[c][h][:, chunk - 1:chunk] + a_last[h] for h in heads]
    for h in heads:
        ct_ref[h] = ct_st[h]
        n_ref[h] = n_st[h]
        m_ref[h:h + 1, :] = jnp.broadcast_to(m_st[h], (1, m_ref.shape[1]))

    a_rows, w_inters, s_mats, vs_t = [], [], [], []
    for c in range(len(chunks)):
        e_mats = [jnp.where(ti <= tj, r_cols[c][h], NEG_INF) for h in heads]
        a_rows.append([jnp.maximum(m_sts[c][h], jnp.max(e_mats[h], axis=0, keepdims=True))
                       for h in heads])
        w_inters.append([jnp.exp(m_sts[c][h] - a_rows[c][h]) for h in heads])
        s_mats.append([kq[c][h] * jnp.exp(e_mats[h] - a_rows[c][h]) for h in heads])
        vs_t.append([_dot(vts[c][h], s_mats[c][h].astype(BF16)) for h in heads])

    y_m = []
    for c, cs in enumerate(chunks):
        for h in heads:
            den = jnp.sum(s_mats[c][h], axis=0, keepdims=True) + w_inters[c][h] * nq[c][h]
            scale = 1.0 / jnp.maximum(jnp.abs(den), jnp.exp(-(brs[c][h] + a_rows[c][h])))
            hh = (vs_t[c][h] + w_inters[c][h] * cq[c][h]) * scale
            hc = hh - jnp.mean(hh, axis=0, keepdims=True)
            var = jnp.mean(hc * hc, axis=0, keepdims=True)
            hn = hc * lax.rsqrt(var + LN_EPS) * mngt_ref[h * dv:(h + 1) * dv, :]
            hmt_ref[h * dv:(h + 1) * dv, cs] = (
                ot_ref[h * dv:(h + 1) * dv, cs].astype(F32) * hn).astype(BF16)
        y_m.append(_dot(wmot_ref[...], hmt_ref[:, cs]))

    for c, cs in enumerate(chunks):
        y_s5 = zt[c][:d_model, :] * _sigmoid(zt[c][d_model:, :])
        mix = gst_ref[:, cs].astype(F32) * y_s5 + gmt_ref[:, cs].astype(F32) * y_m[c]
        out_ref[cs, :] = _dot_tn(mix.astype(BF16), wo_ref[...])
    if emit_state:
        c_out_ref[...] = ct_ref[...]
        n_out_ref[...] = n_ref[...]
        m_out_ref[...] = m_ref[...]


def _mixer(yst, q, k, vt, ot, gst, gmt, gcol, grow, wglut, mngt, wmot, wo, c0, n0, m0,
           *, n_pad, emit_state):
    bsz, d_model, length = gst.shape
    tile = min(TILE_MIX, length)
    chunk = MLSTM_CHUNK
    assert length % tile == 0 and tile % chunk == 0 and mngt.shape[1] == chunk
    d_qk = q.shape[-1]
    d_v = vt.shape[1]
    d_s5 = yst.shape[1]
    dk, dv = d_qk // M_HEADS, d_v // M_HEADS
    grid = (bsz, length // tile)

    def tok(width):
        return pl.BlockSpec((None, tile, width), lambda b, j: (b, j, 0))

    def feat(rows):
        return pl.BlockSpec((None, rows, tile), lambda b, j: (b, 0, j))

    consts = [wglut, mngt, wmot, wo, c0, n0, m0]
    in_specs = ([feat(d_s5), tok(d_qk), tok(d_qk), feat(d_v), feat(d_v), feat(d_model),
                 feat(d_model), tok(LANES), feat(2 * M_HEADS)]
                + [_const_spec(c.shape) for c in consts])
    out_specs = [tok(d_model)]
    out_shape = [jax.ShapeDtypeStruct((bsz, length, d_model), F32)]
    if emit_state:
        out_specs += [pl.BlockSpec((None, M_HEADS, dv, dk), lambda b, j: (b, 0, 0, 0)),
                      pl.BlockSpec((None, M_HEADS, 8, dk), lambda b, j: (b, 0, 0, 0)),
                      pl.BlockSpec((None, 8, dk), lambda b, j: (b, 0, 0))]
        out_shape += [jax.ShapeDtypeStruct((bsz, M_HEADS, dv, dk), F32),
                      jax.ShapeDtypeStruct((bsz, M_HEADS, 8, dk), F32),
                      jax.ShapeDtypeStruct((bsz, 8, dk), F32)]
    kern = functools.partial(_mixer_kernel, tile=tile, chunk=chunk, n_pad=n_pad,
                             emit_state=emit_state, d_model=d_model)
    return pl.pallas_call(
        kern,
        grid=grid,
        in_specs=in_specs,
        out_specs=out_specs,
        out_shape=out_shape,
        scratch_shapes=[pltpu.VMEM((M_HEADS, dv, dk), F32), pltpu.VMEM((M_HEADS, 8, dk), F32),
                        pltpu.VMEM((8, dk), F32), pltpu.VMEM((d_v, tile), BF16)],
        compiler_params=pltpu.CompilerParams(
            dimension_semantics=("arbitrary", "arbitrary"), vmem_limit_bytes=V7X_VMEM_LIMIT),
        name="mixer",
    )(yst, q, k, vt, ot, gst, gmt, gcol, grow, *consts)


def _ffn_kernel(x_ref, pre_ref, g0_ref, b0_ref, g1_ref, b1_ref, wu_ref, bu_ref, wd_ref, g2_ref,
                b2_ref, out_ref, *, alpha, ff_chunk, row_block):
    d_ff = wu_ref.shape[1]
    tile = x_ref.shape[0]
    for r0 in range(0, tile, row_block):
        rows = slice(r0, r0 + row_block)
        h0 = _layer_norm(x_ref[rows, :], g0_ref[...], b0_ref[...])
        h = _layer_norm(alpha * h0 + pre_ref[rows, :], g1_ref[...], b1_ref[...])
        hb = h.astype(BF16)
        acc = alpha * h
        for c in range(d_ff // ff_chunk):
            lo = c * ff_chunk
            a = jnp.maximum(
                _dot(hb, wu_ref[:, lo:lo + ff_chunk]) + bu_ref[:, lo:lo + ff_chunk], 0.0)
            acc = acc + _dot((a * a).astype(BF16), wd_ref[lo:lo + ff_chunk, :])
        out_ref[rows, :] = _layer_norm(acc, g2_ref[...], b2_ref[...])


def _ffn(x, pre, g0, b0, g1, b1, wu, bu, wd, g2, b2, *, alpha):
    rows, d_model = x.shape
    tile = TILE_FFN
    assert rows % tile == 0
    d_ff = wu.shape[1]
    return pl.pallas_call(
        functools.partial(_ffn_kernel, alpha=alpha, ff_chunk=min(1024, d_ff),
                          row_block=min(ROW_BLOCK, tile)),
        grid=(rows // tile,),
        in_specs=[pl.BlockSpec((tile, d_model), lambda i: (i, 0)),
                  pl.BlockSpec((tile, d_model), lambda i: (i, 0)),
                  _const_spec((1, d_model)), _const_spec((1, d_model)),
                  _const_spec((1, d_model)), _const_spec((1, d_model)),
                  _const_spec((d_model, d_ff)), _const_spec((1, d_ff)),
                  _const_spec((d_ff, d_model)), _const_spec((1, d_model)),
                  _const_spec((1, d_model))],
        out_specs=pl.BlockSpec((tile, d_model), lambda i: (i, 0)),
        out_shape=jax.ShapeDtypeStruct((rows, d_model), F32),
        compiler_params=pltpu.CompilerParams(
            dimension_semantics=("arbitrary",), vmem_limit_bytes=V7X_VMEM_LIMIT),
        name="ffn",
    )(x, pre, g0, b0, g1, b1, wu, bu, wd, g2, b2)


def kernel(x, meta_tokens, ln0_g, ln0_b, w_in, b_in, qk_conv_w, qk_conv_b, s5_lambda_re, s5_lambda_im, s5_log_dt, s5_b_re, s5_b_im, s5_c_re, s5_c_im, s5_d, s5_w_glu, m_norm_g, m_w_out, w_o, ln1_g, ln1_b, w_up, b_up, w_down, ln2_g, ln2_b):
    bsz, seq, d_model = x.shape
    depth = w_in.shape[0]
    assert depth == 1, "the meta-token prologue is written for a single layer"
    assert meta_tokens.shape == (N_META, d_model)
    assert seq % TILE_MIX == 0 and seq % S5_CHUNK == 0 and MLSTM_CHUNK >= S5_CHUNK >= N_META
    alpha = (2.0 * depth) ** 0.25
    d_s5 = d_model // 2
    d_qk = qk_conv_w.shape[-1] // 2
    d_v = m_norm_g.shape[-1]
    dk, dv = d_qk // M_HEADS, d_v // M_HEADS
    gate_lo = d_s5 + 2 * d_qk + 2 * d_v
    gate_hi = gate_lo + 2 * M_HEADS
    assert w_in.shape[-1] == gate_hi + 2 * d_model

    row = lambda a: a.reshape(1, -1).astype(F32)
    w = w_in[0]
    bias = b_in[0]
    qk_lo, qk_hi = d_s5, d_s5 + 2 * d_qk
    w_t = jnp.concatenate([w[:, :qk_lo], w[:, qk_hi:gate_lo], w[:, gate_hi:]], axis=1).T.astype(BF16)
    b_t = jnp.concatenate([bias[:qk_lo], bias[qk_hi:gate_lo], bias[gate_hi:]]).reshape(-1, 1).astype(F32)
    w_qk = w[:, qk_lo:qk_hi].astype(BF16)
    b_qk = row(bias[qk_lo:qk_hi])
    w_gate = w[:, gate_lo:gate_hi]
    wgc = jnp.pad(w_gate, ((0, 0), (0, LANES - 2 * M_HEADS))).astype(BF16)
    bgc = jnp.pad(bias[gate_lo:gate_hi], (0, LANES - 2 * M_HEADS)).reshape(1, LANES).astype(F32)
    wgr = w_gate.T.astype(BF16)
    bgr = bias[gate_lo:gate_hi].reshape(2 * M_HEADS, 1).astype(F32)
    g0, b0 = row(ln0_g), row(ln0_b)
    conv_w = qk_conv_w[0].astype(F32)
    conv_b = row(qk_conv_b[0])
    mngt = jnp.broadcast_to(m_norm_g[0].astype(F32)[:, None], (d_v, MLSTM_CHUNK))
    mixer_consts = (s5_w_glu[0].T.astype(BF16), mngt, m_w_out[0].T.astype(BF16),
                    w_o[0].astype(BF16))
    inproj_consts = (g0, b0, w_t, b_t, w_qk, b_qk, conv_w, conv_b, wgc, bgc, wgr, bgr)

    pad = MLSTM_CHUNK - N_META
    x_meta = jnp.concatenate([jnp.zeros((pad, d_model), x.dtype), meta_tokens.astype(x.dtype)])[None]
    ut_m, q_m, k_m, v_m, o_m, gs_m, gm_m, gcol_m, grow_m, carry_m = _inproj(
        x_meta, *inproj_consts, jnp.zeros((CARRY_ROWS, 2 * d_qk), F32), n_pad=pad)
    zero_state = (jnp.zeros((M_HEADS, dv, dk), F32), jnp.zeros((M_HEADS, 8, dk), F32),
                  jnp.zeros((8, dk), F32))
    _, c_m, n_m, m_m = _mixer(
        jnp.zeros((1, d_s5, MLSTM_CHUNK), BF16), q_m, k_m, v_m, o_m, gs_m, gm_m, gcol_m, grow_m,
        *mixer_consts, *zero_state, n_pad=pad, emit_state=True)

    ut, q, k, v, o, gs, gm, gcol, grow, _ = _inproj(x, *inproj_consts, carry_m[0], n_pad=0)
    s5_consts = _s5_weights(s5_lambda_re[0], s5_lambda_im[0], s5_log_dt[0], s5_b_re[0], s5_b_im[0],
                            s5_c_re[0], s5_c_im[0], s5_d[0])
    yst = _s5(ut, ut_m[0, :, MLSTM_CHUNK - S5_CHUNK:], *s5_consts)
    (pre,) = _mixer(yst, q, k, v, o, gs, gm, gcol, grow, *mixer_consts, c_m[0], n_m[0], m_m[0],
                    n_pad=0, emit_state=False)
    out = _ffn(x.reshape(bsz * seq, d_model), pre.reshape(bsz * seq, d_model), g0, b0,
               row(ln1_g[0]), row(ln1_b[0]), w_up[0].astype(BF16), row(b_up[0]),
               w_down[0].astype(BF16), row(ln2_g[0]), row(ln2_b[0]), alpha=alpha)
    return out.reshape(bsz, seq, d_model)
```

```python
import functools
import math

import jax
import jax.numpy as jnp
from jax import lax
from jax.experimental import pallas as pl
from jax.experimental.pallas import tpu as pltpu

F32 = jnp.float32
BF16 = jnp.bfloat16

N_META = 16
S5_GROUP = 16
S5_STATE = 64
M_HEADS = 4
CONV_WIDTH = 4
LN_EPS = 1e-5

LANES = 128
V7X_MXU_WIDTH = 256
S5_CHUNK = LANES
TILE_IN = 512
MLSTM_CHUNK = 256
TILE_MIX = 1024
TILE_FFN = 512
ROW_BLOCK = 256
CARRY_ROWS = 8
V7X_VMEM_LIMIT = 56 * 1024 * 1024

NEG_INF = float("-inf")


def _layer_norm(x, g, b):
    mu = jnp.mean(x, axis=-1, keepdims=True)
    xc = x - mu
    var = jnp.mean(xc * xc, axis=-1, keepdims=True)
    return xc * lax.rsqrt(var + LN_EPS) * g + b


def _sigmoid(x):
    return 0.5 * jnp.tanh(0.5 * x) + 0.5


def _log_sigmoid(x):
    return jnp.minimum(x, 0.0) - jnp.log1p(jnp.exp(-jnp.abs(x)))


def _gelu_tanh(x):
    c = math.sqrt(2.0 / math.pi)
    return x * (0.5 * (1.0 + jnp.tanh(c * (x + 0.044715 * (x * x * x)))))


def _dot(a, b):
    return jnp.dot(a, b, preferred_element_type=F32)


def _dot_nt(a, b):
    return lax.dot_general(a, b, (((1,), (1,)), ((), ())), preferred_element_type=F32)


def _dot_tn(a, b):
    return lax.dot_general(a, b, (((0,), (0,)), ((), ())), preferred_element_type=F32)


def _split3(x):
    hi = x.astype(BF16)
    r = x - hi.astype(F32)
    mid = r.astype(BF16)
    lo = (r - mid.astype(F32)).astype(BF16)
    return hi, mid, lo


def _const_spec(shape):
    nd = len(shape)
    return pl.BlockSpec(shape, lambda *_: (0,) * nd, pipeline_mode=pl.Buffered(1))


def _inproj_kernel(x_ref, g0_ref, b0_ref, wt_ref, bt_ref, w_ref, b_ref, cw_ref, cb_ref,
                   wgc_ref, bgc_ref, wgr_ref, bgr_ref, carry_in_ref,
                   ut_ref, q_ref, k_ref, vt_ref, ot_ref, gst_ref, gmt_ref, gcol_ref, grow_ref,
                   carry_out_ref,
                   ext_ref, *, tile, row_block, n_pad, d_qk, d_s5, d_v, d_model):
    j = pl.program_id(1)

    @pl.when(j == 0)
    def _():
        ext_ref[0:CARRY_ROWS, :] = carry_in_ref[...]

    blocks = [slice(r0, r0 + row_block) for r0 in range(0, tile, row_block)]

    def seg_t(lo, n_rows, rows, hb):
        return _dot_nt(wt_ref[lo:lo + n_rows, :], hb) + bt_ref[lo:lo + n_rows, :]

    hbs = []
    for rows in blocks:
        hb = _layer_norm(x_ref[rows, :], g0_ref[...], b0_ref[...]).astype(BF16)
        hbs.append(hb)
        qk = _dot(hb, w_ref[...]) + b_ref[...]
        if n_pad:
            t_idx = lax.broadcasted_iota(jnp.int32, (row_block, 1), 0) + (j * tile + rows.start)
            qk = jnp.where(t_idx >= n_pad, qk, 0.0)
        ext_ref[CARRY_ROWS + rows.start:CARRY_ROWS + rows.stop, :] = qk
        gcol_ref[rows, :] = _dot(hb, wgc_ref[...]) + bgc_ref[...]
        grow_ref[:, rows] = _dot_nt(wgr_ref[...], hb) + bgr_ref[...]
        ut = seg_t(0, d_s5, rows, hb)
        if n_pad:
            cols = lax.broadcasted_iota(jnp.int32, (1, row_block), 1) + (j * tile + rows.start)
            ut = jnp.where(cols >= n_pad, ut, 0.0)
        ut_ref[:, rows] = ut.astype(BF16)
        vt_ref[:, rows] = seg_t(d_s5, d_v, rows, hb).astype(BF16)
    off = d_s5 + d_v

    ext = ext_ref[...]
    conv = cb_ref[...] + cw_ref[CONV_WIDTH - 1:CONV_WIDTH, :] * ext[CARRY_ROWS:, :]
    for r in range(CONV_WIDTH - 1):
        lag = CONV_WIDTH - 1 - r
        conv = conv + cw_ref[r:r + 1, :] * pltpu.roll(ext, lag, 0)[CARRY_ROWS:, :]
    tail = ext[tile:, :]
    ext_ref[0:CARRY_ROWS, :] = tail
    carry_out_ref[...] = tail
    act = conv * _sigmoid(conv)
    q_ref[...] = act[:, :d_qk].astype(BF16)
    k_ref[...] = (act[:, d_qk:] * ((d_qk // M_HEADS) ** -0.5)).astype(BF16)

    for out_ref, n_rows in ((ot_ref, d_v), (gst_ref, d_model), (gmt_ref, d_model)):
        for rows, hb in zip(blocks, hbs):
            out_ref[:, rows] = _sigmoid(seg_t(off, n_rows, rows, hb)).astype(BF16)
        off += n_rows


def _inproj(x, g0, b0, w_t, b_t, w_qk, b_qk, conv_w, conv_b, wgc, bgc, wgr, bgr, carry_in,
            *, n_pad):
    bsz, length, d_model = x.shape
    tile = min(TILE_IN, length)
    assert length % tile == 0
    d_qk = conv_w.shape[1] // 2
    d_s5 = d_model // 2
    d_v = d_model
    n_t = w_t.shape[0]
    assert n_t == d_s5 + 2 * d_v + 2 * d_model and w_qk.shape[1] == 2 * d_qk
    grid = (bsz, length // tile)

    def tok(width):
        return pl.BlockSpec((None, tile, width), lambda b, j: (b, j, 0))

    def feat(rows):
        return pl.BlockSpec((None, rows, tile), lambda b, j: (b, 0, j))

    def feat_out(rows):
        return jax.ShapeDtypeStruct((bsz, rows, length), BF16)

    kern = functools.partial(_inproj_kernel, tile=tile, row_block=min(ROW_BLOCK, tile), n_pad=n_pad,
                             d_qk=d_qk, d_s5=d_s5, d_v=d_v, d_model=d_model)
    return pl.pallas_call(
        kern,
        grid=grid,
        in_specs=[tok(d_model), _const_spec((1, d_model)), _const_spec((1, d_model)),
                  _const_spec((n_t, d_model)), _const_spec((n_t, 1)),
                  _const_spec((d_model, 2 * d_qk)), _const_spec((1, 2 * d_qk)),
                  _const_spec((CONV_WIDTH, 2 * d_qk)), _const_spec((1, 2 * d_qk)),
                  _const_spec(wgc.shape), _const_spec(bgc.shape), _const_spec(wgr.shape),
                  _const_spec(bgr.shape), _const_spec((CARRY_ROWS, 2 * d_qk))],
        out_specs=[feat(d_s5), tok(d_qk), tok(d_qk), feat(d_v), feat(d_v), feat(d_model),
                   feat(d_model), tok(LANES), feat(2 * M_HEADS),
                   pl.BlockSpec((None, CARRY_ROWS, 2 * d_qk), lambda b, j: (b, 0, 0))],
        out_shape=[feat_out(d_s5),
                   jax.ShapeDtypeStruct((bsz, length, d_qk), BF16),
                   jax.ShapeDtypeStruct((bsz, length, d_qk), BF16),
                   feat_out(d_v), feat_out(d_v), feat_out(d_model), feat_out(d_model),
                   jax.ShapeDtypeStruct((bsz, length, LANES), F32),
                   jax.ShapeDtypeStruct((bsz, 2 * M_HEADS, length), F32),
                   jax.ShapeDtypeStruct((bsz, CARRY_ROWS, 2 * d_qk), F32)],
        scratch_shapes=[pltpu.VMEM((CARRY_ROWS + tile, 2 * d_qk), F32)],
        compiler_params=pltpu.CompilerParams(
            dimension_semantics=("arbitrary", "arbitrary"), vmem_limit_bytes=V7X_VMEM_LIMIT),
        name="inproj",
    )(x, g0, b0, w_t, b_t, w_qk, b_qk, conv_w, conv_b, wgc, bgc, wgr, bgr, carry_in)


def _s5_kernel(ut_ref, um_ref, taps_ref, wst_ref, wc_ref, a1_ref, a2_ref, dsk_ref, yt_ref,
               px_ref, lhs_ref, toe_ref, s_ref, xin_ref, y_ref, ymid_ref, *, n_chunks, bsz):
    t_len = S5_CHUNK
    n_ch = S5_GROUP
    length = n_chunks * t_len
    half = 2 * S5_STATE
    rows_bj = bsz * n_ch
    lane_step = 1024

    r_idx = lax.broadcasted_iota(jnp.int32, (rows_bj, rows_bj), 0)
    q_idx = lax.broadcasted_iota(jnp.int32, (rows_bj, rows_bj), 1)
    perm = jnp.where(q_idx == (r_idx % bsz) * n_ch + r_idx // bsz, 1.0, 0.0).astype(BF16)
    perm_back = jnp.where(q_idx == (r_idx % n_ch) * bsz + r_idx // n_ch, 1.0, 0.0).astype(BF16)
    x_bj = ut_ref[...].reshape(rows_bj, length)
    for lo in range(0, length, lane_step):
        px_ref[:, lo:lo + lane_step] = _dot(perm, x_bj[:, lo:lo + lane_step]).astype(BF16)

    for jj in range(n_ch):
        lhs_ref[0:bsz, jj * t_len:(jj + 1) * t_len] = jnp.broadcast_to(
            um_ref[jj:jj + 1, :], (bsz, t_len))
    for c in range(n_chunks):
        for jj in range(n_ch):
            lhs_ref[(c + 1) * bsz:(c + 2) * bsz, jj * t_len:(jj + 1) * t_len] = (
                px_ref[jj * bsz:(jj + 1) * bsz, c * t_len:(c + 1) * t_len])

    lhs = lhs_ref[...]
    s_ref[...] = _dot(lhs, wst_ref[...])
    a1 = a1_ref[...]
    a2 = a2_ref[...]
    state = jnp.zeros((bsz, 2 * half), F32)
    for c in range(n_chunks + 1):
        xin_ref[c * bsz:(c + 1) * bsz, :] = state[:, :half]
        swapped = jnp.concatenate([state[:, half:], state[:, :half]], axis=1)
        state = a1 * state + a2 * swapped + s_ref[c * bsz:(c + 1) * bsz, :]
    xin = xin_ref[...].astype(BF16)

    s_io = lax.broadcasted_iota(jnp.int32, (t_len, t_len), 0)
    t_io = lax.broadcasted_iota(jnp.int32, (t_len, t_len), 1)
    lower = t_io >= s_io
    cols_per_step = V7X_MXU_WIDTH // t_len
    for i0 in range(0, n_ch, cols_per_step):
        cols = slice(i0 * t_len, (i0 + cols_per_step) * t_len)
        for ii in range(i0, i0 + cols_per_step):
            for jj in range(n_ch):
                tap = jnp.broadcast_to(taps_ref[jj, ii:ii + 1, :], (t_len, t_len))
                blk = pltpu.roll(tap, 0, 1, stride=1, stride_axis=0)
                toe_ref[jj * t_len:(jj + 1) * t_len, ii * t_len:(ii + 1) * t_len] = (
                    jnp.where(lower, blk, 0.0).astype(BF16))
        y_ref[:, cols] = (_dot(lhs, toe_ref[:, cols]) + _dot(xin, wc_ref[:, cols])
                          + lhs[:, cols].astype(F32) * dsk_ref[:, cols])

    for c in range(n_chunks):
        for ii in range(n_ch):
            ymid_ref[ii * bsz:(ii + 1) * bsz, c * t_len:(c + 1) * t_len] = (
                y_ref[(c + 1) * bsz:(c + 2) * bsz, ii * t_len:(ii + 1) * t_len].astype(BF16))
    for lo in range(0, length, lane_step):
        yt_ref[:, :, lo:lo + lane_step] = _dot(
            perm_back, ymid_ref[:, lo:lo + lane_step]).astype(BF16).reshape(bsz, n_ch, lane_step)


def _s5(ut, um, taps, wst, wc, a1, a2, dskip):
    bsz, d_s5, length = ut.shape
    groups = d_s5 // S5_GROUP
    t_len = S5_CHUNK
    n_chunks = length // t_len
    assert length % 1024 == 0
    width = S5_GROUP * t_len
    half = 2 * S5_STATE
    rows = (n_chunks + 1) * bsz

    def grp(*shape):
        return pl.BlockSpec((None,) + shape, lambda g: (g,) + (0,) * len(shape))

    return pl.pallas_call(
        functools.partial(_s5_kernel, n_chunks=n_chunks, bsz=bsz),
        grid=(groups,),
        in_specs=[pl.BlockSpec((bsz, S5_GROUP, length), lambda g: (0, g, 0)),
                  pl.BlockSpec((S5_GROUP, t_len), lambda g: (g, 0)),
                  grp(S5_GROUP, S5_GROUP, t_len), grp(width, 2 * half), grp(half, width),
                  grp(1, 2 * half), grp(1, 2 * half), grp(1, width)],
        out_specs=pl.BlockSpec((bsz, S5_GROUP, length), lambda g: (0, g, 0)),
        out_shape=jax.ShapeDtypeStruct((bsz, d_s5, length), BF16),
        scratch_shapes=[pltpu.VMEM((bsz * S5_GROUP, length), BF16),
                        pltpu.VMEM((rows, width), BF16),
                        pltpu.VMEM((width, width), BF16),
                        pltpu.VMEM((rows, 2 * half), F32),
                        pltpu.VMEM((rows, half), F32),
                        pltpu.VMEM((rows, width), F32),
                        pltpu.VMEM((bsz * S5_GROUP, length), BF16)],
        compiler_params=pltpu.CompilerParams(
            dimension_semantics=("arbitrary",), vmem_limit_bytes=V7X_VMEM_LIMIT),
        name="s5",
    )(ut, um, taps, wst, wc, a1, a2, dskip)


def _s5_weights(lam_re, lam_im, log_dt, b_re, b_im, c_re, c_im, d_skip):
    hp = lax.Precision.HIGHEST
    t_len = S5_CHUNK
    groups, n_state = lam_re.shape
    lr, li = lam_re.astype(F32), lam_im.astype(F32)
    dt = jnp.exp(log_dt.astype(F32))[:, None]
    ar, ai = lr * dt, li * dt
    steps = jnp.arange(t_len + 1, dtype=F32)[:, None, None]
    mag = jnp.exp(ar[None] * steps)
    pr, pi = mag * jnp.cos(ai[None] * steps), mag * jnp.sin(ai[None] * steps)
    nr, ni = pr[1] - 1.0, pi[1]
    den = lr * lr + li * li
    fr, fi = (nr * lr + ni * li) / den, (ni * lr - nr * li) / den
    bre, bim = b_re.astype(F32), b_im.astype(F32)
    bbr = fr[..., None] * bre - fi[..., None] * bim
    bbi = fr[..., None] * bim + fi[..., None] * bre
    cre, cim = c_re.astype(F32), c_im.astype(F32)
    cbr = cre[:, :, None, :] * jnp.transpose(bbr, (0, 2, 1))[:, None] \
        - cim[:, :, None, :] * jnp.transpose(bbi, (0, 2, 1))[:, None]
    cbi = cre[:, :, None, :] * jnp.transpose(bbi, (0, 2, 1))[:, None] \
        + cim[:, :, None, :] * jnp.transpose(bbr, (0, 2, 1))[:, None]
    taps = (jnp.einsum("gijp,dgp->gjid", cbr, pr[:t_len], precision=hp)
            - jnp.einsum("gijp,dgp->gjid", cbi, pi[:t_len], precision=hp))

    prg = jnp.transpose(pr[:t_len][::-1], (1, 0, 2))
    pig = jnp.transpose(pi[:t_len][::-1], (1, 0, 2))
    bjr = jnp.transpose(bbr, (0, 2, 1))
    bji = jnp.transpose(bbi, (0, 2, 1))
    p_re4 = jnp.tile(prg, (1, 1, 4))[:, None]
    p_im4 = jnp.tile(pig, (1, 1, 4))[:, None]
    b_u = jnp.concatenate([bjr, bji, bji, bjr], axis=-1)[:, :, None]
    b_v = jnp.concatenate([-bji, bjr, bjr, -bji], axis=-1)[:, :, None]
    wst = (p_re4 * b_u + p_im4 * b_v).astype(BF16).reshape(groups, S5_GROUP * t_len, 4 * n_state)

    ptr = jnp.transpose(pr[1:], (1, 2, 0))
    pti = jnp.transpose(pi[1:], (1, 2, 0))
    ctr = jnp.transpose(cre, (0, 2, 1))
    cti = jnp.transpose(cim, (0, 2, 1))
    p_re2 = jnp.tile(ptr, (1, 2, 1))[:, :, None]
    p_im2 = jnp.tile(pti, (1, 2, 1))[:, :, None]
    c_u = jnp.concatenate([ctr, -cti], axis=1)[..., None]
    c_v = jnp.concatenate([-cti, -ctr], axis=1)[..., None]
    wc = (c_u * p_re2 + c_v * p_im2).astype(BF16).reshape(groups, 2 * n_state, S5_GROUP * t_len)

    a_re, a_im = pr[t_len], pi[t_len]
    a1 = jnp.concatenate([a_re, a_re, a_re, a_re], axis=-1)[:, None, :]
    a2 = jnp.concatenate([-a_im, a_im, a_im, -a_im], axis=-1)[:, None, :]
    dsk = jnp.repeat(d_skip.astype(F32).reshape(groups, 1, S5_GROUP), t_len, axis=-1)
    return taps, wst, wc, a1, a2, dsk


def _mixer_kernel(yst_ref, q_ref, k_ref, vt_ref, ot_ref, gst_ref, gmt_ref, gcol_ref, grow_ref,
                  wglut_ref, mngt_ref, wmot_ref, wo_ref, c0_ref, n0_ref, m0_ref,
                  *refs, tile, chunk, n_pad, emit_state, d_model):
    if emit_state:
        out_ref, c_out_ref, n_out_ref, m_out_ref, ct_ref, n_ref, m_ref, hmt_ref = refs
    else:
        out_ref, ct_ref, n_ref, m_ref, hmt_ref = refs
    j = pl.program_id(1)
    dk = q_ref.shape[-1] // M_HEADS
    dv = vt_ref.shape[0] // M_HEADS

    @pl.when(j == 0)
    def _():
        ct_ref[...] = c0_ref[...]
        n_ref[...] = n0_ref[...]
        m_ref[...] = m0_ref[...]

    heads = range(M_HEADS)
    chunks = [slice(c0, c0 + chunk) for c0 in range(0, tile, chunk)]
    ti = lax.broadcasted_iota(jnp.int32, (chunk, chunk), 0)
    tj = lax.broadcasted_iota(jnp.int32, (chunk, chunk), 1)
    tri_lower = jnp.where(ti >= tj, 1.0, 0.0).astype(BF16)
    tri_upper = jnp.where(ti <= tj, 1.0, 0.0).astype(BF16)

    qs = [[q_ref[cs, h * dk:(h + 1) * dk] for h in heads] for cs in chunks]
    ks = [[k_ref[cs, h * dk:(h + 1) * dk] for h in heads] for cs in chunks]
    vts = [[vt_ref[h * dv:(h + 1) * dv, cs] for h in heads] for cs in chunks]
    kq = [[_dot_nt(ks[c][h], qs[c][h]) for h in heads] for c in range(len(chunks))]
    zt = [_dot(wglut_ref[...], _gelu_tanh(yst_ref[:, cs].astype(F32)).astype(BF16))
          for cs in chunks]

    m_st = [m_ref[h:h + 1, 0:1] for h in heads]
    ct_st = [ct_ref[h] for h in heads]
    n_st = [n_ref[h] for h in heads]
    m_sts, cq, nq, brs, r_cols = [], [], [], [], []
    for c, cs in enumerate(chunks):
        g_col = gcol_ref[cs, :]
        g_row = grow_ref[:, cs]
        lane = lax.broadcasted_iota(jnp.int32, g_col.shape, 1)
        sub = lax.broadcasted_iota(jnp.int32, g_row.shape, 0)
        lf_col = jnp.where((lane >= M_HEADS) & (lane < 2 * M_HEADS), _log_sigmoid(g_col), 0.0)
        lf_row = jnp.where(sub >= M_HEADS, _log_sigmoid(g_row), 0.0)
        li_col, li_row = g_col, g_row
        if n_pad:
            t_col = lax.broadcasted_iota(jnp.int32, g_col.shape, 0) + (j * tile + cs.start)
            t_row = lax.broadcasted_iota(jnp.int32, g_row.shape, 1) + (j * tile + cs.start)
            lf_col = jnp.where(t_col >= n_pad, lf_col, 0.0)
            lf_row = jnp.where(t_row >= n_pad, lf_row, 0.0)
            li_col = jnp.where(t_col >= n_pad, li_col, NEG_INF)
            li_row = jnp.where(t_row >= n_pad, li_row, NEG_INF)
        b_col = sum(_dot(tri_lower, part) for part in _split3(lf_col))
        b_row = sum(_dot(part, tri_upper) for part in _split3(lf_row))

        brs.append([b_row[M_HEADS + h:M_HEADS + h + 1, :] for h in heads])
        r_cols.append([li_col[:, h:h + 1] - b_col[:, M_HEADS + h:M_HEADS + h + 1] for h in heads])
        r_rows = [li_row[h:h + 1, :] - brs[c][h] for h in heads]

        m_sts.append(m_st)
        cq.append([_dot_nt(ct_st[h].astype(BF16), qs[c][h]) for h in heads])
        nq.append([_dot_nt(n_st[h].astype(BF16), qs[c][h])[0:1, :] for h in heads])
        a_last = [jnp.maximum(m_st[h], jnp.max(r_rows[h], axis=-1, keepdims=True)) for h in heads]
        w_k = [jnp.exp(r_rows[h] - a_last[h]) for h in heads]
        upd = [_dot((vts[c][h].astype(F32) * w_k[h]).astype(BF16), ks[c][h]) for h in heads]
        n_upd = [_dot(jnp.broadcast_to(w_k[h], (8, chunk)).astype(BF16), ks[c][h]) for h in heads]
        decay = [jnp.exp(m_st[h] - a_last[h]) for h in heads]
        ct_st = [decay[h] * ct_st[h] + upd[h] for h in heads]
        n_st = [decay[h] * n_st[h] + n_upd[h] for h in heads]
        m_st = [brs[c][h][:, chunk - 1:chunk] + a_last[h] for h in heads]
    for h in heads:
        ct_ref[h] = ct_st[h]
        n_ref[h] = n_st[h]
        m_ref[h:h + 1, :] = jnp.broadcast_to(m_st[h], (1, m_ref.shape[1]))

    a_rows, w_inters, s_mats, vs_t = [], [], [], []
    for c in range(len(chunks)):
        e_mats = [jnp.where(ti <= tj, r_cols[c][h], NEG_INF) for h in heads]
        a_rows.append([jnp.maximum(m_sts[c][h], jnp.max(e_mats[h], axis=0, keepdims=True))
                       for h in heads])
        w_inters.append([jnp.exp(m_sts[c][h] - a_rows[c][h]) for h in heads])
        s_mats.append([kq[c][h] * jnp.exp(e_mats[h] - a_rows[c][h]) for h in heads])
        vs_t.append([_dot(vts[c][h], s_mats[c][h].astype(BF16)) for h in heads])

    y_m = []
    for c, cs in enumerate(chunks):
        for h in heads:
            den = jnp.sum(s_mats[c][h], axis=0, keepdims=True) + w_inters[c][h] * nq[c][h]
            scale = 1.0 / jnp.maximum(jnp.abs(den), jnp.exp(-(brs[c][h] + a_rows[c][h])))
            hh = (vs_t[c][h] + w_inters[c][h] * cq[c][h]) * scale
            hc = hh - jnp.mean(hh, axis=0, keepdims=True)
            var = jnp.mean(hc * hc, axis=0, keepdims=True)
            hn = hc * lax.rsqrt(var + LN_EPS) * mngt_ref[h * dv:(h + 1) * dv, :]
            hmt_ref[h * dv:(h + 1) * dv, cs] = (
                ot_ref[h * dv:(h + 1) * dv, cs].astype(F32) * hn).astype(BF16)
        y_m.append(_dot(wmot_ref[...], hmt_ref[:, cs]))

    for c, cs in enumerate(chunks):
        y_s5 = zt[c][:d_model, :] * _sigmoid(zt[c][d_model:, :])
        mix = gst_ref[:, cs].astype(F32) * y_s5 + gmt_ref[:, cs].astype(F32) * y_m[c]
        out_ref[cs, :] = _dot_tn(mix.astype(BF16), wo_ref[...])
    if emit_state:
        c_out_ref[...] = ct_ref[...]
        n_out_ref[...] = n_ref[...]
        m_out_ref[...] = m_ref[...]


def _mixer(yst, q, k, vt, ot, gst, gmt, gcol, grow, wglut, mngt, wmot, wo, c0, n0, m0,
           *, n_pad, emit_state):
    bsz, d_model, length = gst.shape
    tile = min(TILE_MIX, length)
    chunk = MLSTM_CHUNK
    assert length % tile == 0 and tile % chunk == 0 and mngt.shape[1] == chunk
    d_qk = q.shape[-1]
    d_v = vt.shape[1]
    d_s5 = yst.shape[1]
    dk, dv = d_qk // M_HEADS, d_v // M_HEADS
    grid = (bsz, length // tile)

    def tok(width):
        return pl.BlockSpec((None, tile, width), lambda b, j: (b, j, 0))

    def feat(rows):
        return pl.BlockSpec((None, rows, tile), lambda b, j: (b, 0, j))

    consts = [wglut, mngt, wmot, wo, c0, n0, m0]
    in_specs = ([feat(d_s5), tok(d_qk), tok(d_qk), feat(d_v), feat(d_v), feat(d_model),
                 feat(d_model), tok(LANES), feat(2 * M_HEADS)]
                + [_const_spec(c.shape) for c in consts])
    out_specs = [tok(d_model)]
    out_shape = [jax.ShapeDtypeStruct((bsz, length, d_model), F32)]
    if emit_state:
        out_specs += [pl.BlockSpec((None, M_HEADS, dv, dk), lambda b, j: (b, 0, 0, 0)),
                      pl.BlockSpec((None, M_HEADS, 8, dk), lambda b, j: (b, 0, 0, 0)),
                      pl.BlockSpec((None, 8, dk), lambda b, j: (b, 0, 0))]
        out_shape += [jax.ShapeDtypeStruct((bsz, M_HEADS, dv, dk), F32),
                      jax.ShapeDtypeStruct((bsz, M_HEADS, 8, dk), F32),
                      jax.ShapeDtypeStruct((bsz, 8, dk), F32)]
    kern = functools.partial(_mixer_kernel, tile=tile, chunk=chunk, n_pad=n_pad,
                             emit_state=emit_state, d_model=d_model)
    return pl.pallas_call(
        kern,
        grid=grid,
        in_specs=in_specs,
        out_specs=out_specs,
        out_shape=out_shape,
        scratch_shapes=[pltpu.VMEM((M_HEADS, dv, dk), F32), pltpu.VMEM((M_HEADS, 8, dk), F32),
                        pltpu.VMEM((8, dk), F32), pltpu.VMEM((d_v, tile), BF16)],
        compiler_params=pltpu.CompilerParams(
            dimension_semantics=("arbitrary", "arbitrary"), vmem_limit_bytes=V7X_VMEM_LIMIT),
        name="mixer",
    )(yst, q, k, vt, ot, gst, gmt, gcol, grow, *consts)


def _ffn_kernel(x_ref, pre_ref, g0_ref, b0_ref, g1_ref, b1_ref, wu_ref, bu_ref, wd_ref, g2_ref,
                b2_ref, out_ref, *, alpha, ff_chunk, row_block):
    d_ff = wu_ref.shape[1]
    tile = x_ref.shape[0]
    for r0 in range(0, tile, row_block):
        rows = slice(r0, r0 + row_block)
        h0 = _layer_norm(x_ref[rows, :], g0_ref[...], b0_ref[...])
        h = _layer_norm(alpha * h0 + pre_ref[rows, :], g1_ref[...], b1_ref[...])
        hb = h.astype(BF16)
        acc = alpha * h
        for c in range(d_ff // ff_chunk):
            lo = c * ff_chunk
            a = jnp.maximum(
                _dot(hb, wu_ref[:, lo:lo + ff_chunk]) + bu_ref[:, lo:lo + ff_chunk], 0.0)
            acc = acc + _dot((a * a).astype(BF16), wd_ref[lo:lo + ff_chunk, :])
        out_ref[rows, :] = _layer_norm(acc, g2_ref[...], b2_ref[...])


def _ffn(x, pre, g0, b0, g1, b1, wu, bu, wd, g2, b2, *, alpha):
    rows, d_model = x.shape
    tile = TILE_FFN
    assert rows % tile == 0
    d_ff = wu.shape[1]
    return pl.pallas_call(
        functools.partial(_ffn_kernel, alpha=alpha, ff_chunk=min(1024, d_ff),
                          row_block=min(ROW_BLOCK, tile)),
        grid=(rows // tile,),
        in_specs=[pl.BlockSpec((tile, d_model), lambda i: (i, 0)),
                  pl.BlockSpec((tile, d_model), lambda i: (i, 0)),
                  _const_spec((1, d_model)), _const_spec((1, d_model)),
                  _const_spec((1, d_model)), _const_spec((1, d_model)),
                  _const_spec((d_model, d_ff)), _const_spec((1, d_ff)),
                  _const_spec((d_ff, d_model)), _const_spec((1, d_model)),
                  _const_spec((1, d_model))],
        out_specs=pl.BlockSpec((tile, d_model), lambda i: (i, 0)),
        out_shape=jax.ShapeDtypeStruct((rows, d_model), F32),
        compiler_params=pltpu.CompilerParams(
            dimension_semantics=("arbitrary",), vmem_limit_bytes=V7X_VMEM_LIMIT),
        name="ffn",
    )(x, pre, g0, b0, g1, b1, wu, bu, wd, g2, b2)


def kernel(x, meta_tokens, ln0_g, ln0_b, w_in, b_in, qk_conv_w, qk_conv_b, s5_lambda_re, s5_lambda_im, s5_log_dt, s5_b_re, s5_b_im, s5_c_re, s5_c_im, s5_d, s5_w_glu, m_norm_g, m_w_out, w_o, ln1_g, ln1_b, w_up, b_up, w_down, ln2_g, ln2_b):
    bsz, seq, d_model = x.shape
    depth = w_in.shape[0]
    assert depth == 1, "the meta-token prologue is written for a single layer"
    assert meta_tokens.shape == (N_META, d_model)
    assert seq % TILE_MIX == 0 and seq % S5_CHUNK == 0 and MLSTM_CHUNK >= S5_CHUNK >= N_META
    alpha = (2.0 * depth) ** 0.25
    d_s5 = d_model // 2
    d_qk = qk_conv_w.shape[-1] // 2
    d_v = m_norm_g.shape[-1]
    dk, dv = d_qk // M_HEADS, d_v // M_HEADS
    gate_lo = d_s5 + 2 * d_qk + 2 * d_v
    gate_hi = gate_lo + 2 * M_HEADS
    assert w_in.shape[-1] == gate_hi + 2 * d_model

    row = lambda a: a.reshape(1, -1).astype(F32)
    w = w_in[0]
    bias = b_in[0]
    qk_lo, qk_hi = d_s5, d_s5 + 2 * d_qk
    w_t = jnp.concatenate([w[:, :qk_lo], w[:, qk_hi:gate_lo], w[:, gate_hi:]], axis=1).T.astype(BF16)
    b_t = jnp.concatenate([bias[:qk_lo], bias[qk_hi:gate_lo], bias[gate_hi:]]).reshape(-1, 1).astype(F32)
    w_qk = w[:, qk_lo:qk_hi].astype(BF16)
    b_qk = row(bias[qk_lo:qk_hi])
    w_gate = w[:, gate_lo:gate_hi]
    wgc = jnp.pad(w_gate, ((0, 0), (0, LANES - 2 * M_HEADS))).astype(BF16)
    bgc = jnp.pad(bias[gate_lo:gate_hi], (0, LANES - 2 * M_HEADS)).reshape(1, LANES).astype(F32)
    wgr = w_gate.T.astype(BF16)
    bgr = bias[gate_lo:gate_hi].reshape(2 * M_HEADS, 1).astype(F32)
    g0, b0 = row(ln0_g), row(ln0_b)
    conv_w = qk_conv_w[0].astype(F32)
    conv_b = row(qk_conv_b[0])
    mngt = jnp.broadcast_to(m_norm_g[0].astype(F32)[:, None], (d_v, MLSTM_CHUNK))
    mixer_consts = (s5_w_glu[0].T.astype(BF16), mngt, m_w_out[0].T.astype(BF16),
                    w_o[0].astype(BF16))
    inproj_consts = (g0, b0, w_t, b_t, w_qk, b_qk, conv_w, conv_b, wgc, bgc, wgr, bgr)

    pad = MLSTM_CHUNK - N_META
    x_meta = jnp.concatenate([jnp.zeros((pad, d_model), x.dtype), meta_tokens.astype(x.dtype)])[None]
    ut_m, q_m, k_m, v_m, o_m, gs_m, gm_m, gcol_m, grow_m, carry_m = _inproj(
        x_meta, *inproj_consts, jnp.zeros((CARRY_ROWS, 2 * d_qk), F32), n_pad=pad)
    zero_state = (jnp.zeros((M_HEADS, dv, dk), F32), jnp.zeros((M_HEADS, 8, dk), F32),
                  jnp.zeros((8, dk), F32))
    _, c_m, n_m, m_m = _mixer(
        jnp.zeros((1, d_s5, MLSTM_CHUNK), BF16), q_m, k_m, v_m, o_m, gs_m, gm_m, gcol_m, grow_m,
        *mixer_consts, *zero_state, n_pad=pad, emit_state=True)

    ut, q, k, v, o, gs, gm, gcol, grow, _ = _inproj(x, *inproj_consts, carry_m[0], n_pad=0)
    s5_consts = _s5_weights(s5_lambda_re[0], s5_lambda_im[0], s5_log_dt[0], s5_b_re[0], s5_b_im[0],
                            s5_c_re[0], s5_c_im[0], s5_d[0])
    yst = _s5(ut, ut_m[0, :, MLSTM_CHUNK - S5_CHUNK:], *s5_consts)
    (pre,) = _mixer(yst, q, k, v, o, gs, gm, gcol, grow, *mixer_consts, c_m[0], n_m[0], m_m[0],
                    n_pad=0, emit_state=False)
    out = _ffn(x.reshape(bsz * seq, d_model), pre.reshape(bsz * seq, d_model), g0, b0,
               row(ln1_g[0]), row(ln1_b[0]), w_up[0].astype(BF16), row(b_up[0]),
               w_down[0].astype(BF16), row(ln2_g[0]), row(ln2_b[0]), alpha=alpha)
    return out.reshape(bsz, seq, d_model)
```

```python
import functools
import math

import jax
import jax.numpy as jnp
from jax import lax
from jax.experimental import pallas as pl
from jax.experimental.pallas import tpu as pltpu

F32 = jnp.float32
BF16 = jnp.bfloat16

N_META = 16
S5_GROUP = 16
S5_STATE = 64
M_HEADS = 4
CONV_WIDTH = 4
LN_EPS = 1e-5

LANES = 128
V7X_MXU_WIDTH = 256
S5_CHUNK = LANES
TILE_IN = 512
MLSTM_CHUNK = 256
TILE_MIX = 1024
TILE_FFN = 512
ROW_BLOCK = 256
CARRY_ROWS = 8
V7X_VMEM_LIMIT = 56 * 1024 * 1024

NEG_INF = float("-inf")


def _layer_norm(x, g, b):
    mu = jnp.mean(x, axis=-1, keepdims=True)
    xc = x - mu
    var = jnp.mean(xc * xc, axis=-1, keepdims=True)
    return xc * lax.rsqrt(var + LN_EPS) * g + b


def _sigmoid(x):
    return 0.5 * jnp.tanh(0.5 * x) + 0.5


def _log_sigmoid(x):
    return jnp.minimum(x, 0.0) - jnp.log1p(jnp.exp(-jnp.abs(x)))


def _gelu_tanh(x):
    c = math.sqrt(2.0 / math.pi)
    return x * (0.5 * (1.0 + jnp.tanh(c * (x + 0.044715 * (x * x * x)))))


def _dot(a, b):
    return jnp.dot(a, b, preferred_element_type=F32)


def _dot_nt(a, b):
    return lax.dot_general(a, b, (((1,), (1,)), ((), ())), preferred_element_type=F32)


def _dot_tn(a, b):
    return lax.dot_general(a, b, (((0,), (0,)), ((), ())), preferred_element_type=F32)


def _split3(x):
    hi = x.astype(BF16)
    r = x - hi.astype(F32)
    mid = r.astype(BF16)
    lo = (r - mid.astype(F32)).astype(BF16)
    return hi, mid, lo


def _const_spec(shape):
    nd = len(shape)
    return pl.BlockSpec(shape, lambda *_: (0,) * nd, pipeline_mode=pl.Buffered(1))


def _inproj_kernel(x_ref, g0_ref, b0_ref, wt_ref, bt_ref, w_ref, b_ref, cw_ref, cb_ref,
                   wgc_ref, bgc_ref, wgr_ref, bgr_ref, carry_in_ref,
                   ut_ref, q_ref, k_ref, vt_ref, ot_ref, gst_ref, gmt_ref, gcol_ref, grow_ref,
                   carry_out_ref,
                   ext_ref, *, tile, row_block, n_pad, d_qk, d_s5, d_v, d_model):
    j = pl.program_id(1)

    @pl.when(j == 0)
    def _():
        ext_ref[0:CARRY_ROWS, :] = carry_in_ref[...]

    blocks = [slice(r0, r0 + row_block) for r0 in range(0, tile, row_block)]

    def seg_t(lo, n_rows, rows, hb):
        return _dot_nt(wt_ref[lo:lo + n_rows, :], hb) + bt_ref[lo:lo + n_rows, :]

    hbs = []
    for rows in blocks:
        hb = _layer_norm(x_ref[rows, :], g0_ref[...], b0_ref[...]).astype(BF16)
        hbs.append(hb)
        qk = _dot(hb, w_ref[...]) + b_ref[...]
        if n_pad:
            t_idx = lax.broadcasted_iota(jnp.int32, (row_block, 1), 0) + (j * tile + rows.start)
            qk = jnp.where(t_idx >= n_pad, qk, 0.0)
        ext_ref[CARRY_ROWS + rows.start:CARRY_ROWS + rows.stop, :] = qk
        gcol_ref[rows, :] = _dot(hb, wgc_ref[...]) + bgc_ref[...]
        grow_ref[:, rows] = _dot_nt(wgr_ref[...], hb) + bgr_ref[...]
        ut = seg_t(0, d_s5, rows, hb)
        if n_pad:
            cols = lax.broadcasted_iota(jnp.int32, (1, row_block), 1) + (j * tile + rows.start)
            ut = jnp.where(cols >= n_pad, ut, 0.0)
        ut_ref[:, rows] = ut.astype(BF16)
        vt_ref[:, rows] = seg_t(d_s5, d_v, rows, hb).astype(BF16)
    off = d_s5 + d_v

    ext = ext_ref[...]
    conv = cb_ref[...] + cw_ref[CONV_WIDTH - 1:CONV_WIDTH, :] * ext[CARRY_ROWS:, :]
    for r in range(CONV_WIDTH - 1):
        lag = CONV_WIDTH - 1 - r
        conv = conv + cw_ref[r:r + 1, :] * pltpu.roll(ext, lag, 0)[CARRY_ROWS:, :]
    tail = ext[tile:, :]
    ext_ref[0:CARRY_ROWS, :] = tail
    carry_out_ref[...] = tail
    act = conv * _sigmoid(conv)
    q_ref[...] = act[:, :d_qk].astype(BF16)
    k_ref[...] = (act[:, d_qk:] * ((d_qk // M_HEADS) ** -0.5)).astype(BF16)

    for out_ref, n_rows in ((ot_ref, d_v), (gst_ref, d_model), (gmt_ref, d_model)):
        for rows, hb in zip(blocks, hbs):
            out_ref[:, rows] = _sigmoid(seg_t(off, n_rows, rows, hb)).astype(BF16)
        off += n_rows


def _inproj(x, g0, b0, w_t, b_t, w_qk, b_qk, conv_w, conv_b, wgc, bgc, wgr, bgr, carry_in,
            *, n_pad):
    bsz, length, d_model = x.shape
    tile = min(TILE_IN, length)
    assert length % tile == 0
    d_qk = conv_w.shape[1] // 2
    d_s5 = d_model // 2
    d_v = d_model
    n_t = w_t.shape[0]
    assert n_t == d_s5 + 2 * d_v + 2 * d_model and w_qk.shape[1] == 2 * d_qk
    grid = (bsz, length // tile)

    def tok(width):
        return pl.BlockSpec((None, tile, width), lambda b, j: (b, j, 0))

    def feat(rows):
        return pl.BlockSpec((None, rows, tile), lambda b, j: (b, 0, j))

    def feat_out(rows):
        return jax.ShapeDtypeStruct((bsz, rows, length), BF16)

    kern = functools.partial(_inproj_kernel, tile=tile, row_block=min(ROW_BLOCK, tile), n_pad=n_pad,
                             d_qk=d_qk, d_s5=d_s5, d_v=d_v, d_model=d_model)
    return pl.pallas_call(
        kern,
        grid=grid,
        in_specs=[tok(d_model), _const_spec((1, d_model)), _const_spec((1, d_model)),
                  _const_spec((n_t, d_model)), _const_spec((n_t, 1)),
                  _const_spec((d_model, 2 * d_qk)), _const_spec((1, 2 * d_qk)),
                  _const_spec((CONV_WIDTH, 2 * d_qk)), _const_spec((1, 2 * d_qk)),
                  _const_spec(wgc.shape), _const_spec(bgc.shape), _const_spec(wgr.shape),
                  _const_spec(bgr.shape), _const_spec((CARRY_ROWS, 2 * d_qk))],
        out_specs=[feat(d_s5), tok(d_qk), tok(d_qk), feat(d_v), feat(d_v), feat(d_model),
                   feat(d_model), tok(LANES), feat(2 * M_HEADS),
                   pl.BlockSpec((None, CARRY_ROWS, 2 * d_qk), lambda b, j: (b, 0, 0))],
        out_shape=[feat_out(d_s5),
                   jax.ShapeDtypeStruct((bsz, length, d_qk), BF16),
                   jax.ShapeDtypeStruct((bsz, length, d_qk), BF16),
                   feat_out(d_v), feat_out(d_v), feat_out(d_model), feat_out(d_model),
                   jax.ShapeDtypeStruct((bsz, length, LANES), F32),
                   jax.ShapeDtypeStruct((bsz, 2 * M_HEADS, length), F32),
                   jax.ShapeDtypeStruct((bsz, CARRY_ROWS, 2 * d_qk), F32)],
        scratch_shapes=[pltpu.VMEM((CARRY_ROWS + tile, 2 * d_qk), F32)],
        compiler_params=pltpu.CompilerParams(
            dimension_semantics=("arbitrary", "arbitrary"), vmem_limit_bytes=V7X_VMEM_LIMIT),
        name="inproj",
    )(x, g0, b0, w_t, b_t, w_qk, b_qk, conv_w, conv_b, wgc, bgc, wgr, bgr, carry_in)


def _s5_kernel(ut_ref, um_ref, taps_ref, wst_ref, wc_ref, a1_ref, a2_ref, dsk_ref, yt_ref,
               px_ref, lhs_ref, toe_ref, s_ref, xin_ref, y_ref, ymid_ref, *, n_chunks, bsz):
    t_len = S5_CHUNK
    n_ch = S5_GROUP
    length = n_chunks * t_len
    half = 2 * S5_STATE
    rows_bj = bsz * n_ch
    lane_step = 1024

    r_idx = lax.broadcasted_iota(jnp.int32, (rows_bj, rows_bj), 0)
    q_idx = lax.broadcasted_iota(jnp.int32, (rows_bj, rows_bj), 1)
    perm = jnp.where(q_idx == (r_idx % bsz) * n_ch + r_idx // bsz, 1.0, 0.0).astype(BF16)
    perm_back = jnp.where(q_idx == (r_idx % n_ch) * bsz + r_idx // n_ch, 1.0, 0.0).astype(BF16)
    x_bj = ut_ref[...].reshape(rows_bj, length)
    for lo in range(0, length, lane_step):
        px_ref[:, lo:lo + lane_step] = _dot(perm, x_bj[:, lo:lo + lane_step]).astype(BF16)

    for jj in range(n_ch):
        lhs_ref[0:bsz, jj * t_len:(jj + 1) * t_len] = jnp.broadcast_to(
            um_ref[jj:jj + 1, :], (bsz, t_len))
    for c in range(n_chunks):
        for jj in range(n_ch):
            lhs_ref[(c + 1) * bsz:(c + 2) * bsz, jj * t_len:(jj + 1) * t_len] = (
                px_ref[jj * bsz:(jj + 1) * bsz, c * t_len:(c + 1) * t_len])

    lhs = lhs_ref[...]
    wst = wst_ref[...].reshape(n_ch * t_len, 2 * half)
    s_ref[...] = _dot(lhs, wst)
    a1 = a1_ref[...]
    a2 = a2_ref[...]
    state = jnp.zeros((bsz, 2 * half), F32)
    for c in range(n_chunks + 1):
        xin_ref[c * bsz:(c + 1) * bsz, :] = state[:, :half]
        swapped = jnp.concatenate([state[:, half:], state[:, :half]], axis=1)
        state = a1 * state + a2 * swapped + s_ref[c * bsz:(c + 1) * bsz, :]
    xin = xin_ref[...].astype(BF16)

    s_io = lax.broadcasted_iota(jnp.int32, (t_len, t_len), 0)
    t_io = lax.broadcasted_iota(jnp.int32, (t_len, t_len), 1)
    lower = t_io >= s_io
    cols_per_step = V7X_MXU_WIDTH // t_len
    for i0 in range(0, n_ch, cols_per_step):
        cols = slice(i0 * t_len, (i0 + cols_per_step) * t_len)
        for ii in range(i0, i0 + cols_per_step):
            for jj in range(n_ch):
                tap = jnp.broadcast_to(taps_ref[jj, ii:ii + 1, :], (t_len, t_len))
                blk = pltpu.roll(tap, 0, 1, stride=1, stride_axis=0)
                toe_ref[jj * t_len:(jj + 1) * t_len, ii * t_len:(ii + 1) * t_len] = (
                    jnp.where(lower, blk, 0.0).astype(BF16))
        y_ref[:, cols] = (_dot(lhs, toe_ref[:, cols]) + _dot(xin, wc_ref[:, cols])
                          + lhs[:, cols].astype(F32) * dsk_ref[:, cols])

    for c in range(n_chunks):
        for ii in range(n_ch):
            ymid_ref[ii * bsz:(ii + 1) * bsz, c * t_len:(c + 1) * t_len] = (
                y_ref[(c + 1) * bsz:(c + 2) * bsz, ii * t_len:(ii + 1) * t_len].astype(BF16))
    for lo in range(0, length, lane_step):
        yt_ref[:, :, lo:lo + lane_step] = _dot(
            perm_back, ymid_ref[:, lo:lo + lane_step]).astype(BF16).reshape(bsz, n_ch, lane_step)


def _s5(ut, um, taps, wst, wc, a1, a2, dskip):
    bsz, d_s5, length = ut.shape
    groups = d_s5 // S5_GROUP
    t_len = S5_CHUNK
    n_chunks = length // t_len
    assert length % 1024 == 0
    width = S5_GROUP * t_len
    half = 2 * S5_STATE
    rows = (n_chunks + 1) * bsz

    def grp(*shape):
        return pl.BlockSpec((None,) + shape, lambda g: (g,) + (0,) * len(shape))

    return pl.pallas_call(
        functools.partial(_s5_kernel, n_chunks=n_chunks, bsz=bsz),
        grid=(groups,),
        in_specs=[pl.BlockSpec((bsz, S5_GROUP, length), lambda g: (0, g, 0)),
                  pl.BlockSpec((S5_GROUP, t_len), lambda g: (g, 0)),
                  grp(S5_GROUP, S5_GROUP, t_len), grp(S5_GROUP, t_len, 2 * half), grp(half, width),
                  grp(1, 2 * half), grp(1, 2 * half), grp(1, width)],
        out_specs=pl.BlockSpec((bsz, S5_GROUP, length), lambda g: (0, g, 0)),
        out_shape=jax.ShapeDtypeStruct((bsz, d_s5, length), BF16),
        scratch_shapes=[pltpu.VMEM((bsz * S5_GROUP, length), BF16),
                        pltpu.VMEM((rows, width), BF16),
                        pltpu.VMEM((width, width), BF16),
                        pltpu.VMEM((rows, 2 * half), F32),
                        pltpu.VMEM((rows, half), F32),
                        pltpu.VMEM((rows, width), F32),
                        pltpu.VMEM((bsz * S5_GROUP, length), BF16)],
        compiler_params=pltpu.CompilerParams(
            dimension_semantics=("arbitrary",), vmem_limit_bytes=V7X_VMEM_LIMIT),
        name="s5",
    )(ut, um, taps, wst, wc, a1, a2, dskip)


def _s5_weights(lam_re, lam_im, log_dt, b_re, b_im, c_re, c_im, d_skip):
    hp = lax.Precision.HIGHEST
    t_len = S5_CHUNK
    groups, n_state = lam_re.shape
    lr, li = lam_re.astype(F32), lam_im.astype(F32)
    dt = jnp.exp(log_dt.astype(F32))[:, None]
    ar, ai = lr * dt, li * dt
    steps = jnp.arange(t_len + 1, dtype=F32)[:, None, None]
    mag = jnp.exp(ar[None] * steps)
    pr, pi = mag * jnp.cos(ai[None] * steps), mag * jnp.sin(ai[None] * steps)
    nr, ni = pr[1] - 1.0, pi[1]
    den = lr * lr + li * li
    fr, fi = (nr * lr + ni * li) / den, (ni * lr - nr * li) / den
    bre, bim = b_re.astype(F32), b_im.astype(F32)
    bbr = fr[..., None] * bre - fi[..., None] * bim
    bbi = fr[..., None] * bim + fi[..., None] * bre
    cre, cim = c_re.astype(F32), c_im.astype(F32)
    cbr = cre[:, :, None, :] * jnp.transpose(bbr, (0, 2, 1))[:, None] \
        - cim[:, :, None, :] * jnp.transpose(bbi, (0, 2, 1))[:, None]
    cbi = cre[:, :, None, :] * jnp.transpose(bbi, (0, 2, 1))[:, None] \
        + cim[:, :, None, :] * jnp.transpose(bbr, (0, 2, 1))[:, None]
    taps = (jnp.einsum("gijp,dgp->gjid", cbr, pr[:t_len], precision=hp)
            - jnp.einsum("gijp,dgp->gjid", cbi, pi[:t_len], precision=hp))

    prg = jnp.transpose(pr[:t_len][::-1], (1, 0, 2))
    pig = jnp.transpose(pi[:t_len][::-1], (1, 0, 2))
    bjr = jnp.transpose(bbr, (0, 2, 1))
    bji = jnp.transpose(bbi, (0, 2, 1))
    p_re4 = jnp.tile(prg, (1, 1, 4))[:, None]
    p_im4 = jnp.tile(pig, (1, 1, 4))[:, None]
    b_u = jnp.concatenate([bjr, bji, bji, bjr], axis=-1)[:, :, None]
    b_v = jnp.concatenate([-bji, bjr, bjr, -bji], axis=-1)[:, :, None]
    wst = (p_re4 * b_u + p_im4 * b_v).astype(BF16)

    ptr = jnp.transpose(pr[1:], (1, 2, 0))
    pti = jnp.transpose(pi[1:], (1, 2, 0))
    ctr = jnp.transpose(cre, (0, 2, 1))
    cti = jnp.transpose(cim, (0, 2, 1))
    p_re2 = jnp.tile(ptr, (1, 2, 1))[:, :, None]
    p_im2 = jnp.tile(pti, (1, 2, 1))[:, :, None]
    c_u = jnp.concatenate([ctr, -cti], axis=1)[..., None]
    c_v = jnp.concatenate([-cti, -ctr], axis=1)[..., None]
    wc = (c_u * p_re2 + c_v * p_im2).astype(BF16).reshape(groups, 2 * n_state, S5_GROUP * t_len)

    a_re, a_im = pr[t_len], pi[t_len]
    a1 = jnp.concatenate([a_re, a_re, a_re, a_re], axis=-1)[:, None, :]
    a2 = jnp.concatenate([-a_im, a_im, a_im, -a_im], axis=-1)[:, None, :]
    dsk = jnp.repeat(d_skip.astype(F32).reshape(groups, 1, S5_GROUP), t_len, axis=-1)
    return taps, wst, wc, a1, a2, dsk


def _mixer_kernel(yst_ref, q_ref, k_ref, vt_ref, ot_ref, gst_ref, gmt_ref, gcol_ref, grow_ref,
                  wglut_ref, mngt_ref, wmot_ref, wo_ref, c0_ref, n0_ref, m0_ref,
                  *refs, tile, chunk, n_pad, emit_state, d_model):
    if emit_state:
        out_ref, c_out_ref, n_out_ref, m_out_ref, ct_ref, n_ref, m_ref, hmt_ref = refs
    else:
        out_ref, ct_ref, n_ref, m_ref, hmt_ref = refs
    j = pl.program_id(1)
    dk = q_ref.shape[-1] // M_HEADS
    dv = vt_ref.shape[0] // M_HEADS

    @pl.when(j == 0)
    def _():
        ct_ref[...] = c0_ref[...]
        n_ref[...] = n0_ref[...]
        m_ref[...] = m0_ref[...]

    heads = range(M_HEADS)
    chunks = [slice(c0, c0 + chunk) for c0 in range(0, tile, chunk)]
    ti = lax.broadcasted_iota(jnp.int32, (chunk, chunk), 0)
    tj = lax.broadcasted_iota(jnp.int32, (chunk, chunk), 1)
    tri_lower = jnp.where(ti >= tj, 1.0, 0.0).astype(BF16)
    tri_upper = jnp.where(ti <= tj, 1.0, 0.0).astype(BF16)

    qs = [[q_ref[cs, h * dk:(h + 1) * dk] for h in heads] for cs in chunks]
    ks = [[k_ref[cs, h * dk:(h + 1) * dk] for h in heads] for cs in chunks]
    vts = [[vt_ref[h * dv:(h + 1) * dv, cs] for h in heads] for cs in chunks]
    state = {"m": [m_ref[h:h + 1, 0:1] for h in heads],
             "ct": [ct_ref[h] for h in heads],
             "n": [n_ref[h] for h in heads]}
    kq, zt, m_sts, cq, nq, brs, r_cols = {}, {}, {}, {}, {}, {}, {}

    def stage1(c):
        cs = chunks[c]
        m_st, ct_st, n_st = state["m"], state["ct"], state["n"]
        kq[c] = [_dot_nt(ks[c][h], qs[c][h]) for h in heads]
        zt[c] = _dot(wglut_ref[...], _gelu_tanh(yst_ref[:, cs].astype(F32)).astype(BF16))
        g_col = gcol_ref[cs, :]
        g_row = grow_ref[:, cs]
        lane = lax.broadcasted_iota(jnp.int32, g_col.shape, 1)
        sub = lax.broadcasted_iota(jnp.int32, g_row.shape, 0)
        lf_col = jnp.where((lane >= M_HEADS) & (lane < 2 * M_HEADS), _log_sigmoid(g_col), 0.0)
        lf_row = jnp.where(sub >= M_HEADS, _log_sigmoid(g_row), 0.0)
        li_col, li_row = g_col, g_row
        if n_pad:
            t_col = lax.broadcasted_iota(jnp.int32, g_col.shape, 0) + (j * tile + cs.start)
            t_row = lax.broadcasted_iota(jnp.int32, g_row.shape, 1) + (j * tile + cs.start)
            lf_col = jnp.where(t_col >= n_pad, lf_col, 0.0)
            lf_row = jnp.where(t_row >= n_pad, lf_row, 0.0)
            li_col = jnp.where(t_col >= n_pad, li_col, NEG_INF)
            li_row = jnp.where(t_row >= n_pad, li_row, NEG_INF)
        b_col = sum(_dot(tri_lower, part) for part in _split3(lf_col))
        b_row = sum(_dot(part, tri_upper) for part in _split3(lf_row))

        brs[c] = [b_row[M_HEADS + h:M_HEADS + h + 1, :] for h in heads]
        r_cols[c] = [li_col[:, h:h + 1] - b_col[:, M_HEADS + h:M_HEADS + h + 1] for h in heads]
        r_rows = [li_row[h:h + 1, :] - brs[c][h] for h in heads]

        m_sts[c] = m_st
        cq[c] = [_dot_nt(ct_st[h].astype(BF16), qs[c][h]) for h in heads]
        nq[c] = [_dot_nt(n_st[h].astype(BF16), qs[c][h])[0:1, :] for h in heads]
        a_last = [jnp.maximum(m_st[h], jnp.max(r_rows[h], axis=-1, keepdims=True)) for h in heads]
        w_k = [jnp.exp(r_rows[h] - a_last[h]) for h in heads]
        upd = [_dot((vts[c][h].astype(F32) * w_k[h]).astype(BF16), ks[c][h]) for h in heads]
        n_upd = [_dot(jnp.broadcast_to(w_k[h], (8, chunk)).astype(BF16), ks[c][h]) for h in heads]
        decay = [jnp.exp(m_st[h] - a_last[h]) for h in heads]
        state["ct"] = [decay[h] * ct_st[h] + upd[h] for h in heads]
        state["n"] = [decay[h] * n_st[h] + n_upd[h] for h in heads]
        state["m"] = [brs[c][h][:, chunk - 1:chunk] + a_last[h] for h in heads]
        if c == len(chunks) - 1:
            for h in heads:
                ct_ref[h] = state["ct"][h]
                n_ref[h] = state["n"][h]
                m_ref[h:h + 1, :] = jnp.broadcast_to(state["m"][h], (1, m_ref.shape[1]))

    a_rows, w_inters, s_mats, vs_t, y_m = {}, {}, {}, {}, {}

    def stage2(c):
        e_mats = [jnp.where(ti <= tj, r_cols[c][h], NEG_INF) for h in heads]
        a_rows[c] = [jnp.maximum(m_sts[c][h], jnp.max(e_mats[h], axis=0, keepdims=True))
                     for h in heads]
        w_inters[c] = [jnp.exp(m_sts[c][h] - a_rows[c][h]) for h in heads]
        s_mats[c] = [kq[c][h] * jnp.exp(e_mats[h] - a_rows[c][h]) for h in heads]
        vs_t[c] = [_dot(vts[c][h], s_mats[c][h].astype(BF16)) for h in heads]

    def stage3(c):
        cs = chunks[c]
        for h in heads:
            den = jnp.sum(s_mats[c][h], axis=0, keepdims=True) + w_inters[c][h] * nq[c][h]
            scale = 1.0 / jnp.maximum(jnp.abs(den), jnp.exp(-(brs[c][h] + a_rows[c][h])))
            hh = (vs_t[c][h] + w_inters[c][h] * cq[c][h]) * scale
            hc = hh - jnp.mean(hh, axis=0, keepdims=True)
            var = jnp.mean(hc * hc, axis=0, keepdims=True)
            hn = hc * lax.rsqrt(var + LN_EPS) * mngt_ref[h * dv:(h + 1) * dv, :]
            hmt_ref[h * dv:(h + 1) * dv, cs] = (
                ot_ref[h * dv:(h + 1) * dv, cs].astype(F32) * hn).astype(BF16)
        y_m[c] = _dot(wmot_ref[...], hmt_ref[:, cs])

    def stage4(c):
        cs = chunks[c]
        y_s5 = zt[c][:d_model, :] * _sigmoid(zt[c][d_model:, :])
        mix = gst_ref[:, cs].astype(F32) * y_s5 + gmt_ref[:, cs].astype(F32) * y_m[c]
        out_ref[cs, :] = _dot_tn(mix.astype(BF16), wo_ref[...])

    stages = (stage1, stage2, stage3, stage4)
    for wave in range(len(chunks) + len(stages) - 1):
        for depth, stage in enumerate(stages):
            c = wave - depth
            if 0 <= c < len(chunks):
                stage(c)
    if emit_state:
        c_out_ref[...] = ct_ref[...]
        n_out_ref[...] = n_ref[...]
        m_out_ref[...] = m_ref[...]


def _mixer(yst, q, k, vt, ot, gst, gmt, gcol, grow, wglut, mngt, wmot, wo, c0, n0, m0,
           *, n_pad, emit_state):
    bsz, d_model, length = gst.shape
    tile = min(TILE_MIX, length)
    chunk = MLSTM_CHUNK
    assert length % tile == 0 and tile % chunk == 0 and mngt.shape[1] == chunk
    d_qk = q.shape[-1]
    d_v = vt.shape[1]
    d_s5 = yst.shape[1]
    dk, dv = d_qk // M_HEADS, d_v // M_HEADS
    grid = (bsz, length // tile)

    def tok(width):
        return pl.BlockSpec((None, tile, width), lambda b, j: (b, j, 0))

    def feat(rows):
        return pl.BlockSpec((None, rows, tile), lambda b, j: (b, 0, j))

    consts = [wglut, mngt, wmot, wo, c0, n0, m0]
    in_specs = ([feat(d_s5), tok(d_qk), tok(d_qk), feat(d_v), feat(d_v), feat(d_model),
                 feat(d_model), tok(LANES), feat(2 * M_HEADS)]
                + [_const_spec(c.shape) for c in consts])
    out_specs = [tok(d_model)]
    out_shape = [jax.ShapeDtypeStruct((bsz, length, d_model), F32)]
    if emit_state:
        out_specs += [pl.BlockSpec((None, M_HEADS, dv, dk), lambda b, j: (b, 0, 0, 0)),
                      pl.BlockSpec((None, M_HEADS, 8, dk), lambda b, j: (b, 0, 0, 0)),
                      pl.BlockSpec((None, 8, dk), lambda b, j: (b, 0, 0))]
        out_shape += [jax.ShapeDtypeStruct((bsz, M_HEADS, dv, dk), F32),
                      jax.ShapeDtypeStruct((bsz, M_HEADS, 8, dk), F32),
                      jax.ShapeDtypeStruct((bsz, 8, dk), F32)]
    kern = functools.partial(_mixer_kernel, tile=tile, chunk=chunk, n_pad=n_pad,
                             emit_state=emit_state, d_model=d_model)
    return pl.pallas_call(
        kern,
        grid=grid,
        in_specs=in_specs,
        out_specs=out_specs,
        out_shape=out_shape,
        scratch_shapes=[pltpu.VMEM((M_HEADS, dv, dk), F32), pltpu.VMEM((M_HEADS, 8, dk), F32),
                        pltpu.VMEM((8, dk), F32), pltpu.VMEM((d_v, tile), BF16)],
        compiler_params=pltpu.CompilerParams(
            dimension_semantics=("arbitrary", "arbitrary"), vmem_limit_bytes=V7X_VMEM_LIMIT),
        name="mixer",
    )(yst, q, k, vt, ot, gst, gmt, gcol, grow, *consts)


def _ffn_kernel(x_ref, pre_ref, g0_ref, b0_ref, g1_ref, b1_ref, wu_ref, bu_ref, wd_ref, g2_ref,
                b2_ref, out_ref, *, alpha, ff_chunk, row_block):
    d_ff = wu_ref.shape[1]
    tile = x_ref.shape[0]
    for r0 in range(0, tile, row_block):
        rows = slice(r0, r0 + row_block)
        h0 = _layer_norm(x_ref[rows, :], g0_ref[...], b0_ref[...])
        h = _layer_norm(alpha * h0 + pre_ref[rows, :], g1_ref[...], b1_ref[...])
        hb = h.astype(BF16)
        acc = alpha * h
        for c in range(d_ff // ff_chunk):
            lo = c * ff_chunk
            a = jnp.maximum(
                _dot(hb, wu_ref[:, lo:lo + ff_chunk]) + bu_ref[:, lo:lo + ff_chunk], 0.0)
            acc = acc + _dot((a * a).astype(BF16), wd_ref[lo:lo + ff_chunk, :])
        out_ref[rows, :] = _layer_norm(acc, g2_ref[...], b2_ref[...])


def _ffn(x, pre, g0, b0, g1, b1, wu, bu, wd, g2, b2, *, alpha):
    rows, d_model = x.shape
    tile = TILE_FFN
    assert rows % tile == 0
    d_ff = wu.shape[1]
    return pl.pallas_call(
        functools.partial(_ffn_kernel, alpha=alpha, ff_chunk=min(2048, d_ff),
                          row_block=min(ROW_BLOCK, tile)),
        grid=(rows // tile,),
        in_specs=[pl.BlockSpec((tile, d_model), lambda i: (i, 0)),
                  pl.BlockSpec((tile, d_model), lambda i: (i, 0)),
                  _const_spec((1, d_model)), _const_spec((1, d_model)),
                  _const_spec((1, d_model)), _const_spec((1, d_model)),
                  _const_spec((d_model, d_ff)), _const_spec((1, d_ff)),
                  _const_spec((d_ff, d_model)), _const_spec((1, d_model)),
                  _const_spec((1, d_model))],
        out_specs=pl.BlockSpec((tile, d_model), lambda i: (i, 0)),
        out_shape=jax.ShapeDtypeStruct((rows, d_model), F32),
        compiler_params=pltpu.CompilerParams(
            dimension_semantics=("arbitrary",), vmem_limit_bytes=V7X_VMEM_LIMIT),
        name="ffn",
    )(x, pre, g0, b0, g1, b1, wu, bu, wd, g2, b2)


def kernel(x, meta_tokens, ln0_g, ln0_b, w_in, b_in, qk_conv_w, qk_conv_b, s5_lambda_re, s5_lambda_im, s5_log_dt, s5_b_re, s5_b_im, s5_c_re, s5_c_im, s5_d, s5_w_glu, m_norm_g, m_w_out, w_o, ln1_g, ln1_b, w_up, b_up, w_down, ln2_g, ln2_b):
    bsz, seq, d_model = x.shape
    depth = w_in.shape[0]
    assert depth == 1, "the meta-token prologue is written for a single layer"
    assert meta_tokens.shape == (N_META, d_model)
    assert seq % TILE_MIX == 0 and seq % S5_CHUNK == 0 and MLSTM_CHUNK >= S5_CHUNK >= N_META
    alpha = (2.0 * depth) ** 0.25
    d_s5 = d_model // 2
    d_qk = qk_conv_w.shape[-1] // 2
    d_v = m_norm_g.shape[-1]
    dk, dv = d_qk // M_HEADS, d_v // M_HEADS
    gate_lo = d_s5 + 2 * d_qk + 2 * d_v
    gate_hi = gate_lo + 2 * M_HEADS
    assert w_in.shape[-1] == gate_hi + 2 * d_model

    row = lambda a: a.reshape(1, -1).astype(F32)
    w = w_in[0]
    bias = b_in[0]
    qk_lo, qk_hi = d_s5, d_s5 + 2 * d_qk
    w_t = jnp.concatenate([w[:, :qk_lo], w[:, qk_hi:gate_lo], w[:, gate_hi:]], axis=1).T.astype(BF16)
    b_t = jnp.concatenate([bias[:qk_lo], bias[qk_hi:gate_lo], bias[gate_hi:]]).reshape(-1, 1).astype(F32)
    w_qk = w[:, qk_lo:qk_hi].astype(BF16)
    b_qk = row(bias[qk_lo:qk_hi])
    w_gate = w[:, gate_lo:gate_hi]
    wgc = jnp.pad(w_gate, ((0, 0), (0, LANES - 2 * M_HEADS))).astype(BF16)
    bgc = jnp.pad(bias[gate_lo:gate_hi], (0, LANES - 2 * M_HEADS)).reshape(1, LANES).astype(F32)
    wgr = w_gate.T.astype(BF16)
    bgr = bias[gate_lo:gate_hi].reshape(2 * M_HEADS, 1).astype(F32)
    g0, b0 = row(ln0_g), row(ln0_b)
    conv_w = qk_conv_w[0].astype(F32)
    conv_b = row(qk_conv_b[0])
    mngt = jnp.broadcast_to(m_norm_g[0].astype(F32)[:, None], (d_v, MLSTM_CHUNK))
    mixer_consts = (s5_w_glu[0].T.astype(BF16), mngt, m_w_out[0].T.astype(BF16),
                    w_o[0].astype(BF16))
    inproj_consts = (g0, b0, w_t, b_t, w_qk, b_qk, conv_w, conv_b, wgc, bgc, wgr, bgr)

    pad = MLSTM_CHUNK - N_META
    x_meta = jnp.concatenate([jnp.zeros((pad, d_model), x.dtype), meta_tokens.astype(x.dtype)])[None]
    ut_m, q_m, k_m, v_m, o_m, gs_m, gm_m, gcol_m, grow_m, carry_m = _inproj(
        x_meta, *inproj_consts, jnp.zeros((CARRY_ROWS, 2 * d_qk), F32), n_pad=pad)
    zero_state = (jnp.zeros((M_HEADS, dv, dk), F32), jnp.zeros((M_HEADS, 8, dk), F32),
                  jnp.zeros((8, dk), F32))
    _, c_m, n_m, m_m = _mixer(
        jnp.zeros((1, d_s5, MLSTM_CHUNK), BF16), q_m, k_m, v_m, o_m, gs_m, gm_m, gcol_m, grow_m,
        *mixer_consts, *zero_state, n_pad=pad, emit_state=True)

    ut, q, k, v, o, gs, gm, gcol, grow, _ = _inproj(x, *inproj_consts, carry_m[0], n_pad=0)
    s5_consts = _s5_weights(s5_lambda_re[0], s5_lambda_im[0], s5_log_dt[0], s5_b_re[0], s5_b_im[0],
                            s5_c_re[0], s5_c_im[0], s5_d[0])
    yst = _s5(ut, ut_m[0, :, MLSTM_CHUNK - S5_CHUNK:], *s5_consts)
    (pre,) = _mixer(yst, q, k, v, o, gs, gm, gcol, grow, *mixer_consts, c_m[0], n_m[0], m_m[0],
                    n_pad=0, emit_state=False)
    out = _ffn(x.reshape(bsz * seq, d_model), pre.reshape(bsz * seq, d_model), g0, b0,
               row(ln1_g[0]), row(ln1_b[0]), w_up[0].astype(BF16), row(b_up[0]),
               w_down[0].astype(BF16), row(ln2_g[0]), row(ln2_b[0]), alpha=alpha)
    return out.reshape(bsz, seq, d_model)
```

```python
import functools
import math

import jax
import jax.numpy as jnp
from jax import lax
from jax.experimental import pallas as pl
from jax.experimental.pallas import tpu as pltpu

F32 = jnp.float32
BF16 = jnp.bfloat16

N_META = 16
S5_GROUP = 16
S5_STATE = 64
M_HEADS = 4
CONV_WIDTH = 4
LN_EPS = 1e-5

LANES = 128
V7X_MXU_WIDTH = 256
S5_CHUNK = LANES
TILE_IN = 512
MLSTM_CHUNK = 256
TILE_MIX = 1024
TILE_FFN = 512
ROW_BLOCK = 256
CARRY_ROWS = 8
V7X_VMEM_LIMIT = 56 * 1024 * 1024

NEG_INF = float("-inf")


def _layer_norm(x, g, b):
    mu = jnp.mean(x, axis=-1, keepdims=True)
    xc = x - mu
    var = jnp.mean(xc * xc, axis=-1, keepdims=True)
    return xc * lax.rsqrt(var + LN_EPS) * g + b


def _sigmoid(x):
    return 0.5 * jnp.tanh(0.5 * x) + 0.5


def _log_sigmoid(x):
    return jnp.minimum(x, 0.0) - jnp.log1p(jnp.exp(-jnp.abs(x)))


def _gelu_tanh(x):
    c = math.sqrt(2.0 / math.pi)
    return x * (0.5 * (1.0 + jnp.tanh(c * (x + 0.044715 * (x * x * x)))))


def _dot(a, b):
    return jnp.dot(a, b, preferred_element_type=F32)


def _dot_nt(a, b):
    return lax.dot_general(a, b, (((1,), (1,)), ((), ())), preferred_element_type=F32)


def _dot_tn(a, b):
    return lax.dot_general(a, b, (((0,), (0,)), ((), ())), preferred_element_type=F32)


def _split3(x):
    hi = x.astype(BF16)
    r = x - hi.astype(F32)
    mid = r.astype(BF16)
    lo = (r - mid.astype(F32)).astype(BF16)
    return hi, mid, lo


def _const_spec(shape):
    nd = len(shape)
    return pl.BlockSpec(shape, lambda *_: (0,) * nd, pipeline_mode=pl.Buffered(1))


def _inproj_kernel(x_ref, g0_ref, b0_ref, wt_ref, bt_ref, w_ref, b_ref, cw_ref, cb_ref,
                   wgc_ref, bgc_ref, wgr_ref, bgr_ref, carry_in_ref,
                   ut_ref, q_ref, k_ref, vt_ref, ot_ref, gst_ref, gmt_ref, gcol_ref, grow_ref,
                   carry_out_ref,
                   ext_ref, *, tile, row_block, n_pad, d_qk, d_s5, d_v, d_model):
    j = pl.program_id(1)

    @pl.when(j == 0)
    def _():
        ext_ref[0:CARRY_ROWS, :] = carry_in_ref[...]

    blocks = [slice(r0, r0 + row_block) for r0 in range(0, tile, row_block)]

    def seg_t(lo, n_rows, rows, hb):
        return _dot_nt(wt_ref[lo:lo + n_rows, :], hb) + bt_ref[lo:lo + n_rows, :]

    hbs = []
    for rows in blocks:
        hb = _layer_norm(x_ref[rows, :], g0_ref[...], b0_ref[...]).astype(BF16)
        hbs.append(hb)
        qk = _dot(hb, w_ref[...]) + b_ref[...]
        if n_pad:
            t_idx = lax.broadcasted_iota(jnp.int32, (row_block, 1), 0) + (j * tile + rows.start)
            qk = jnp.where(t_idx >= n_pad, qk, 0.0)
        ext_ref[CARRY_ROWS + rows.start:CARRY_ROWS + rows.stop, :] = qk
        gcol_ref[rows, :] = _dot(hb, wgc_ref[...]) + bgc_ref[...]
        grow_ref[:, rows] = _dot_nt(wgr_ref[...], hb) + bgr_ref[...]
        ut = seg_t(0, d_s5, rows, hb)
        if n_pad:
            cols = lax.broadcasted_iota(jnp.int32, (1, row_block), 1) + (j * tile + rows.start)
            ut = jnp.where(cols >= n_pad, ut, 0.0)
        ut_ref[:, rows] = ut.astype(BF16)
        vt_ref[:, rows] = seg_t(d_s5, d_v, rows, hb).astype(BF16)
    off = d_s5 + d_v

    ext = ext_ref[...]
    conv = cb_ref[...] + cw_ref[CONV_WIDTH - 1:CONV_WIDTH, :] * ext[CARRY_ROWS:, :]
    for r in range(CONV_WIDTH - 1):
        lag = CONV_WIDTH - 1 - r
        conv = conv + cw_ref[r:r + 1, :] * pltpu.roll(ext, lag, 0)[CARRY_ROWS:, :]
    tail = ext[tile:, :]
    ext_ref[0:CARRY_ROWS, :] = tail
    carry_out_ref[...] = tail
    act = conv * _sigmoid(conv)
    q_ref[...] = act[:, :d_qk].astype(BF16)
    k_ref[...] = (act[:, d_qk:] * ((d_qk // M_HEADS) ** -0.5)).astype(BF16)

    for out_ref, n_rows in ((ot_ref, d_v), (gst_ref, d_model), (gmt_ref, d_model)):
        for rows, hb in zip(blocks, hbs):
            out_ref[:, rows] = _sigmoid(seg_t(off, n_rows, rows, hb)).astype(BF16)
        off += n_rows


def _inproj(x, g0, b0, w_t, b_t, w_qk, b_qk, conv_w, conv_b, wgc, bgc, wgr, bgr, carry_in,
            *, n_pad):
    bsz, length, d_model = x.shape
    tile = min(TILE_IN, length)
    assert length % tile == 0
    d_qk = conv_w.shape[1] // 2
    d_s5 = d_model // 2
    d_v = d_model
    n_t = w_t.shape[0]
    assert n_t == d_s5 + 2 * d_v + 2 * d_model and w_qk.shape[1] == 2 * d_qk
    grid = (bsz, length // tile)

    def tok(width):
        return pl.BlockSpec((None, tile, width), lambda b, j: (b, j, 0))

    def feat(rows):
        return pl.BlockSpec((None, rows, tile), lambda b, j: (b, 0, j))

    def feat_out(rows):
        return jax.ShapeDtypeStruct((bsz, rows, length), BF16)

    kern = functools.partial(_inproj_kernel, tile=tile, row_block=min(ROW_BLOCK, tile), n_pad=n_pad,
                             d_qk=d_qk, d_s5=d_s5, d_v=d_v, d_model=d_model)
    return pl.pallas_call(
        kern,
        grid=grid,
        in_specs=[tok(d_model), _const_spec((1, d_model)), _const_spec((1, d_model)),
                  _const_spec((n_t, d_model)), _const_spec((n_t, 1)),
                  _const_spec((d_model, 2 * d_qk)), _const_spec((1, 2 * d_qk)),
                  _const_spec((CONV_WIDTH, 2 * d_qk)), _const_spec((1, 2 * d_qk)),
                  _const_spec(wgc.shape), _const_spec(bgc.shape), _const_spec(wgr.shape),
                  _const_spec(bgr.shape), _const_spec((CARRY_ROWS, 2 * d_qk))],
        out_specs=[feat(d_s5), tok(d_qk), tok(d_qk), feat(d_v), feat(d_v), feat(d_model),
                   feat(d_model), tok(LANES), feat(2 * M_HEADS),
                   pl.BlockSpec((None, CARRY_ROWS, 2 * d_qk), lambda b, j: (b, 0, 0))],
        out_shape=[feat_out(d_s5),
                   jax.ShapeDtypeStruct((bsz, length, d_qk), BF16),
                   jax.ShapeDtypeStruct((bsz, length, d_qk), BF16),
                   feat_out(d_v), feat_out(d_v), feat_out(d_model), feat_out(d_model),
                   jax.ShapeDtypeStruct((bsz, length, LANES), F32),
                   jax.ShapeDtypeStruct((bsz, 2 * M_HEADS, length), F32),
                   jax.ShapeDtypeStruct((bsz, CARRY_ROWS, 2 * d_qk), F32)],
        scratch_shapes=[pltpu.VMEM((CARRY_ROWS + tile, 2 * d_qk), F32)],
        compiler_params=pltpu.CompilerParams(
            dimension_semantics=("arbitrary", "arbitrary"), vmem_limit_bytes=V7X_VMEM_LIMIT),
        name="inproj",
    )(x, g0, b0, w_t, b_t, w_qk, b_qk, conv_w, conv_b, wgc, bgc, wgr, bgr, carry_in)


def _s5_kernel(ut_ref, um_ref, taps_ref, wst_ref, wc_ref, a1_ref, a2_ref, dsk_ref, yt_ref,
               px_ref, lhs_ref, toe_ref, s_ref, xin_ref, y_ref, ymid_ref, *, n_chunks, bsz):
    t_len = S5_CHUNK
    n_ch = S5_GROUP
    length = n_chunks * t_len
    half = 2 * S5_STATE
    rows_bj = bsz * n_ch
    lane_step = 1024

    r_idx = lax.broadcasted_iota(jnp.int32, (rows_bj, rows_bj), 0)
    q_idx = lax.broadcasted_iota(jnp.int32, (rows_bj, rows_bj), 1)
    perm = jnp.where(q_idx == (r_idx % bsz) * n_ch + r_idx // bsz, 1.0, 0.0).astype(BF16)
    perm_back = jnp.where(q_idx == (r_idx % n_ch) * bsz + r_idx // n_ch, 1.0, 0.0).astype(BF16)
    x_bj = ut_ref[...].reshape(rows_bj, length)
    for lo in range(0, length, lane_step):
        px_ref[:, lo:lo + lane_step] = _dot(perm, x_bj[:, lo:lo + lane_step]).astype(BF16)

    for jj in range(n_ch):
        lhs_ref[0:bsz, jj * t_len:(jj + 1) * t_len] = jnp.broadcast_to(
            um_ref[jj:jj + 1, :], (bsz, t_len))
    for c in range(n_chunks):
        for jj in range(n_ch):
            lhs_ref[(c + 1) * bsz:(c + 2) * bsz, jj * t_len:(jj + 1) * t_len] = (
                px_ref[jj * bsz:(jj + 1) * bsz, c * t_len:(c + 1) * t_len])

    lhs = lhs_ref[...]
    wst = wst_ref[...].reshape(n_ch * t_len, 2 * half)
    s_ref[...] = _dot(lhs, wst)
    a1 = a1_ref[...]
    a2 = a2_ref[...]
    state = jnp.zeros((bsz, 2 * half), F32)
    for c in range(n_chunks + 1):
        xin_ref[c * bsz:(c + 1) * bsz, :] = state[:, :half]
        swapped = jnp.concatenate([state[:, half:], state[:, :half]], axis=1)
        state = a1 * state + a2 * swapped + s_ref[c * bsz:(c + 1) * bsz, :]
    xin = xin_ref[...].astype(BF16)

    s_io = lax.broadcasted_iota(jnp.int32, (t_len, t_len), 0)
    t_io = lax.broadcasted_iota(jnp.int32, (t_len, t_len), 1)
    lower = t_io >= s_io
    cols_per_step = V7X_MXU_WIDTH // t_len
    for i0 in range(0, n_ch, cols_per_step):
        cols = slice(i0 * t_len, (i0 + cols_per_step) * t_len)
        for ii in range(i0, i0 + cols_per_step):
            for jj in range(n_ch):
                tap = jnp.broadcast_to(taps_ref[jj, ii:ii + 1, :], (t_len, t_len))
                blk = pltpu.roll(tap, 0, 1, stride=1, stride_axis=0)
                toe_ref[jj * t_len:(jj + 1) * t_len, ii * t_len:(ii + 1) * t_len] = (
                    jnp.where(lower, blk, 0.0).astype(BF16))
        y_ref[:, cols] = (_dot(lhs, toe_ref[:, cols]) + _dot(xin, wc_ref[:, cols])
                          + lhs[:, cols].astype(F32) * dsk_ref[:, cols])

    for c in range(n_chunks):
        for ii in range(n_ch):
            ymid_ref[ii * bsz:(ii + 1) * bsz, c * t_len:(c + 1) * t_len] = (
                y_ref[(c + 1) * bsz:(c + 2) * bsz, ii * t_len:(ii + 1) * t_len].astype(BF16))
    for lo in range(0, length, lane_step):
        yt_ref[:, :, lo:lo + lane_step] = _dot(
            perm_back, ymid_ref[:, lo:lo + lane_step]).astype(BF16).reshape(bsz, n_ch, lane_step)


def _s5(ut, um, taps, wst, wc, a1, a2, dskip):
    bsz, d_s5, length = ut.shape
    groups = d_s5 // S5_GROUP
    t_len = S5_CHUNK
    n_chunks = length // t_len
    assert length % 1024 == 0
    width = S5_GROUP * t_len
    half = 2 * S5_STATE
    rows = (n_chunks + 1) * bsz

    def grp(*shape):
        return pl.BlockSpec((None,) + shape, lambda g: (g,) + (0,) * len(shape))

    return pl.pallas_call(
        functools.partial(_s5_kernel, n_chunks=n_chunks, bsz=bsz),
        grid=(groups,),
        in_specs=[pl.BlockSpec((bsz, S5_GROUP, length), lambda g: (0, g, 0)),
                  pl.BlockSpec((S5_GROUP, t_len), lambda g: (g, 0)),
                  grp(S5_GROUP, S5_GROUP, t_len), grp(S5_GROUP, t_len, 2 * half), grp(half, width),
                  grp(1, 2 * half), grp(1, 2 * half), grp(1, width)],
        out_specs=pl.BlockSpec((bsz, S5_GROUP, length), lambda g: (0, g, 0)),
        out_shape=jax.ShapeDtypeStruct((bsz, d_s5, length), BF16),
        scratch_shapes=[pltpu.VMEM((bsz * S5_GROUP, length), BF16),
                        pltpu.VMEM((rows, width), BF16),
                        pltpu.VMEM((width, width), BF16),
                        pltpu.VMEM((rows, 2 * half), F32),
                        pltpu.VMEM((rows, half), F32),
                        pltpu.VMEM((rows, width), F32),
                        pltpu.VMEM((bsz * S5_GROUP, length), BF16)],
        compiler_params=pltpu.CompilerParams(
            dimension_semantics=("arbitrary",), vmem_limit_bytes=V7X_VMEM_LIMIT),
        name="s5",
    )(ut, um, taps, wst, wc, a1, a2, dskip)


def _s5_weights(lam_re, lam_im, log_dt, b_re, b_im, c_re, c_im, d_skip):
    hp = lax.Precision.HIGHEST
    t_len = S5_CHUNK
    groups, n_state = lam_re.shape
    lr, li = lam_re.astype(F32), lam_im.astype(F32)
    dt = jnp.exp(log_dt.astype(F32))[:, None]
    ar, ai = lr * dt, li * dt
    steps = jnp.arange(t_len + 1, dtype=F32)[:, None, None]
    mag = jnp.exp(ar[None] * steps)
    pr, pi = mag * jnp.cos(ai[None] * steps), mag * jnp.sin(ai[None] * steps)
    nr, ni = pr[1] - 1.0, pi[1]
    den = lr * lr + li * li
    fr, fi = (nr * lr + ni * li) / den, (ni * lr - nr * li) / den
    bre, bim = b_re.astype(F32), b_im.astype(F32)
    bbr = fr[..., None] * bre - fi[..., None] * bim
    bbi = fr[..., None] * bim + fi[..., None] * bre
    cre, cim = c_re.astype(F32), c_im.astype(F32)
    cbr = cre[:, :, None, :] * jnp.transpose(bbr, (0, 2, 1))[:, None] \
        - cim[:, :, None, :] * jnp.transpose(bbi, (0, 2, 1))[:, None]
    cbi = cre[:, :, None, :] * jnp.transpose(bbi, (0, 2, 1))[:, None] \
        + cim[:, :, None, :] * jnp.transpose(bbr, (0, 2, 1))[:, None]
    taps = (jnp.einsum("gijp,dgp->gjid", cbr, pr[:t_len], precision=hp)
            - jnp.einsum("gijp,dgp->gjid", cbi, pi[:t_len], precision=hp))

    prg = jnp.transpose(pr[:t_len][::-1], (1, 0, 2))
    pig = jnp.transpose(pi[:t_len][::-1], (1, 0, 2))
    bjr = jnp.transpose(bbr, (0, 2, 1))
    bji = jnp.transpose(bbi, (0, 2, 1))
    p_re4 = jnp.tile(prg, (1, 1, 4))[:, None]
    p_im4 = jnp.tile(pig, (1, 1, 4))[:, None]
    b_u = jnp.concatenate([bjr, bji, bji, bjr], axis=-1)[:, :, None]
    b_v = jnp.concatenate([-bji, bjr, bjr, -bji], axis=-1)[:, :, None]
    wst = (p_re4 * b_u + p_im4 * b_v).astype(BF16)

    ptr = jnp.transpose(pr[1:], (1, 2, 0))
    pti = jnp.transpose(pi[1:], (1, 2, 0))
    ctr = jnp.transpose(cre, (0, 2, 1))
    cti = jnp.transpose(cim, (0, 2, 1))
    p_re2 = jnp.tile(ptr, (1, 2, S5_GROUP))
    p_im2 = jnp.tile(pti, (1, 2, S5_GROUP))
    c_u = jnp.repeat(jnp.concatenate([ctr, -cti], axis=1), t_len, axis=-1)
    c_v = jnp.repeat(jnp.concatenate([-cti, -ctr], axis=1), t_len, axis=-1)
    wc = (c_u * p_re2 + c_v * p_im2).astype(BF16)

    a_re, a_im = pr[t_len], pi[t_len]
    a1 = jnp.concatenate([a_re, a_re, a_re, a_re], axis=-1)[:, None, :]
    a2 = jnp.concatenate([-a_im, a_im, a_im, -a_im], axis=-1)[:, None, :]
    dsk = jnp.repeat(d_skip.astype(F32).reshape(groups, 1, S5_GROUP), t_len, axis=-1)
    return taps, wst, wc, a1, a2, dsk


def _mixer_kernel(yst_ref, q_ref, k_ref, vt_ref, ot_ref, gst_ref, gmt_ref, gcol_ref, grow_ref,
                  wglut_ref, mngt_ref, wmot_ref, wo_ref, c0_ref, n0_ref, m0_ref,
                  *refs, tile, chunk, n_pad, emit_state, d_model):
    if emit_state:
        out_ref, c_out_ref, n_out_ref, m_out_ref, ct_ref, n_ref, m_ref, hmt_ref = refs
    else:
        out_ref, ct_ref, n_ref, m_ref, hmt_ref = refs
    j = pl.program_id(1)
    dk = q_ref.shape[-1] // M_HEADS
    dv = vt_ref.shape[0] // M_HEADS

    @pl.when(j == 0)
    def _():
        ct_ref[...] = c0_ref[...]
        n_ref[...] = n0_ref[...]
        m_ref[...] = m0_ref[...]

    heads = range(M_HEADS)
    chunks = [slice(c0, c0 + chunk) for c0 in range(0, tile, chunk)]
    ti = lax.broadcasted_iota(jnp.int32, (chunk, chunk), 0)
    tj = lax.broadcasted_iota(jnp.int32, (chunk, chunk), 1)
    tri_lower = jnp.where(ti >= tj, 1.0, 0.0).astype(BF16)
    tri_upper = jnp.where(ti <= tj, 1.0, 0.0).astype(BF16)

    qs = [[q_ref[cs, h * dk:(h + 1) * dk] for h in heads] for cs in chunks]
    ks = [[k_ref[cs, h * dk:(h + 1) * dk] for h in heads] for cs in chunks]
    vts = [[vt_ref[h * dv:(h + 1) * dv, cs] for h in heads] for cs in chunks]
    state = {"m": [m_ref[h:h + 1, 0:1] for h in heads],
             "ct": [ct_ref[h] for h in heads],
             "n": [n_ref[h] for h in heads]}
    kq, zt, m_sts, cq, nq, brs, r_cols = {}, {}, {}, {}, {}, {}, {}

    def stage1(c):
        cs = chunks[c]
        m_st, ct_st, n_st = state["m"], state["ct"], state["n"]
        kq[c] = [_dot_nt(ks[c][h], qs[c][h]) for h in heads]
        zt[c] = _dot(wglut_ref[...], _gelu_tanh(yst_ref[:, cs].astype(F32)).astype(BF16))
        g_col = gcol_ref[cs, :]
        g_row = grow_ref[:, cs]
        lane = lax.broadcasted_iota(jnp.int32, g_col.shape, 1)
        sub = lax.broadcasted_iota(jnp.int32, g_row.shape, 0)
        lf_col = jnp.where((lane >= M_HEADS) & (lane < 2 * M_HEADS), _log_sigmoid(g_col), 0.0)
        lf_row = jnp.where(sub >= M_HEADS, _log_sigmoid(g_row), 0.0)
        li_col, li_row = g_col, g_row
        if n_pad:
            t_col = lax.broadcasted_iota(jnp.int32, g_col.shape, 0) + (j * tile + cs.start)
            t_row = lax.broadcasted_iota(jnp.int32, g_row.shape, 1) + (j * tile + cs.start)
            lf_col = jnp.where(t_col >= n_pad, lf_col, 0.0)
            lf_row = jnp.where(t_row >= n_pad, lf_row, 0.0)
            li_col = jnp.where(t_col >= n_pad, li_col, NEG_INF)
            li_row = jnp.where(t_row >= n_pad, li_row, NEG_INF)
        b_col = sum(_dot(tri_lower, part) for part in _split3(lf_col))
        b_row = sum(_dot(part, tri_upper) for part in _split3(lf_row))

        brs[c] = [b_row[M_HEADS + h:M_HEADS + h + 1, :] for h in heads]
        r_cols[c] = [li_col[:, h:h + 1] - b_col[:, M_HEADS + h:M_HEADS + h + 1] for h in heads]
        r_rows = [li_row[h:h + 1, :] - brs[c][h] for h in heads]

        m_sts[c] = m_st
        cq[c] = [_dot_nt(ct_st[h].astype(BF16), qs[c][h]) for h in heads]
        nq[c] = [_dot_nt(n_st[h].astype(BF16), qs[c][h])[0:1, :] for h in heads]
        a_last = [jnp.maximum(m_st[h], jnp.max(r_rows[h], axis=-1, keepdims=True)) for h in heads]
        w_k = [jnp.exp(r_rows[h] - a_last[h]) for h in heads]
        upd = [_dot((vts[c][h].astype(F32) * w_k[h]).astype(BF16), ks[c][h]) for h in heads]
        n_upd = [_dot(jnp.broadcast_to(w_k[h], (8, chunk)).astype(BF16), ks[c][h]) for h in heads]
        decay = [jnp.exp(m_st[h] - a_last[h]) for h in heads]
        state["ct"] = [decay[h] * ct_st[h] + upd[h] for h in heads]
        state["n"] = [decay[h] * n_st[h] + n_upd[h] for h in heads]
        state["m"] = [brs[c][h][:, chunk - 1:chunk] + a_last[h] for h in heads]
        if c == len(chunks) - 1:
            for h in heads:
                ct_ref[h] = state["ct"][h]
                n_ref[h] = state["n"][h]
                m_ref[h:h + 1, :] = jnp.broadcast_to(state["m"][h], (1, m_ref.shape[1]))

    a_rows, w_inters, s_mats, vs_t, y_m = {}, {}, {}, {}, {}

    def stage2(c):
        e_mats = [jnp.where(ti <= tj, r_cols[c][h], NEG_INF) for h in heads]
        a_rows[c] = [jnp.maximum(m_sts[c][h], jnp.max(e_mats[h], axis=0, keepdims=True))
                     for h in heads]
        w_inters[c] = [jnp.exp(m_sts[c][h] - a_rows[c][h]) for h in heads]
        s_mats[c] = [kq[c][h] * jnp.exp(e_mats[h] - a_rows[c][h]) for h in heads]
        vs_t[c] = [_dot(vts[c][h], s_mats[c][h].astype(BF16)) for h in heads]

    def stage3(c):
        cs = chunks[c]
        for h in heads:
            den = jnp.sum(s_mats[c][h], axis=0, keepdims=True) + w_inters[c][h] * nq[c][h]
            scale = 1.0 / jnp.maximum(jnp.abs(den), jnp.exp(-(brs[c][h] + a_rows[c][h])))
            hh = (vs_t[c][h] + w_inters[c][h] * cq[c][h]) * scale
            hc = hh - jnp.mean(hh, axis=0, keepdims=True)
            var = jnp.mean(hc * hc, axis=0, keepdims=True)
            hn = hc * lax.rsqrt(var + LN_EPS) * mngt_ref[h * dv:(h + 1) * dv, :]
            hmt_ref[h * dv:(h + 1) * dv, cs] = (
                ot_ref[h * dv:(h + 1) * dv, cs].astype(F32) * hn).astype(BF16)
        y_m[c] = _dot(wmot_ref[...], hmt_ref[:, cs])

    def stage4(c):
        cs = chunks[c]
        y_s5 = zt[c][:d_model, :] * _sigmoid(zt[c][d_model:, :])
        mix = gst_ref[:, cs].astype(F32) * y_s5 + gmt_ref[:, cs].astype(F32) * y_m[c]
        out_ref[cs, :] = _dot_tn(mix.astype(BF16), wo_ref[...])

    stages = (stage1, stage2, stage3, stage4)
    for wave in range(len(chunks) + len(stages) - 1):
        for depth, stage in enumerate(stages):
            c = wave - depth
            if 0 <= c < len(chunks):
                stage(c)
    if emit_state:
        c_out_ref[...] = ct_ref[...]
        n_out_ref[...] = n_ref[...]
        m_out_ref[...] = m_ref[...]


def _mixer(yst, q, k, vt, ot, gst, gmt, gcol, grow, wglut, mngt, wmot, wo, c0, n0, m0,
           *, n_pad, emit_state):
    bsz, d_model, length = gst.shape
    tile = min(TILE_MIX, length)
    chunk = MLSTM_CHUNK
    assert length % tile == 0 and tile % chunk == 0 and mngt.shape[1] == chunk
    d_qk = q.shape[-1]
    d_v = vt.shape[1]
    d_s5 = yst.shape[1]
    dk, dv = d_qk // M_HEADS, d_v // M_HEADS
    grid = (bsz, length // tile)

    def tok(width):
        return pl.BlockSpec((None, tile, width), lambda b, j: (b, j, 0))

    def feat(rows):
        return pl.BlockSpec((None, rows, tile), lambda b, j: (b, 0, j))

    consts = [wglut, mngt, wmot, wo, c0, n0, m0]
    in_specs = ([feat(d_s5), tok(d_qk), tok(d_qk), feat(d_v), feat(d_v), feat(d_model),
                 feat(d_model), tok(LANES), feat(2 * M_HEADS)]
                + [_const_spec(c.shape) for c in consts])
    out_specs = [tok(d_model)]
    out_shape = [jax.ShapeDtypeStruct((bsz, length, d_model), F32)]
    if emit_state:
        out_specs += [pl.BlockSpec((None, M_HEADS, dv, dk), lambda b, j: (b, 0, 0, 0)),
                      pl.BlockSpec((None, M_HEADS, 8, dk), lambda b, j: (b, 0, 0, 0)),
                      pl.BlockSpec((None, 8, dk), lambda b, j: (b, 0, 0))]
        out_shape += [jax.ShapeDtypeStruct((bsz, M_HEADS, dv, dk), F32),
                      jax.ShapeDtypeStruct((bsz, M_HEADS, 8, dk), F32),
                      jax.ShapeDtypeStruct((bsz, 8, dk), F32)]
    kern = functools.partial(_mixer_kernel, tile=tile, chunk=chunk, n_pad=n_pad,
                             emit_state=emit_state, d_model=d_model)
    return pl.pallas_call(
        kern,
        grid=grid,
        in_specs=in_specs,
        out_specs=out_specs,
        out_shape=out_shape,
        scratch_shapes=[pltpu.VMEM((M_HEADS, dv, dk), F32), pltpu.VMEM((M_HEADS, 8, dk), F32),
                        pltpu.VMEM((8, dk), F32), pltpu.VMEM((d_v, tile), BF16)],
        compiler_params=pltpu.CompilerParams(
            dimension_semantics=("arbitrary", "arbitrary"), vmem_limit_bytes=V7X_VMEM_LIMIT),
        name="mixer",
    )(yst, q, k, vt, ot, gst, gmt, gcol, grow, *consts)


def _ffn_kernel(x_ref, pre_ref, g0_ref, b0_ref, g1_ref, b1_ref, wu_ref, bu_ref, wd_ref, g2_ref,
                b2_ref, out_ref, *, alpha, ff_chunk, row_block):
    d_ff = wu_ref.shape[1]
    tile = x_ref.shape[0]
    for r0 in range(0, tile, row_block):
        rows = slice(r0, r0 + row_block)
        h0 = _layer_norm(x_ref[rows, :], g0_ref[...], b0_ref[...])
        h = _layer_norm(alpha * h0 + pre_ref[rows, :], g1_ref[...], b1_ref[...])
        hb = h.astype(BF16)
        acc = alpha * h
        for c in range(d_ff // ff_chunk):
            lo = c * ff_chunk
            a = jnp.maximum(
                _dot(hb, wu_ref[:, lo:lo + ff_chunk]) + bu_ref[:, lo:lo + ff_chunk], 0.0)
            acc = acc + _dot((a * a).astype(BF16), wd_ref[lo:lo + ff_chunk, :])
        out_ref[rows, :] = _layer_norm(acc, g2_ref[...], b2_ref[...])


def _ffn(x, pre, g0, b0, g1, b1, wu, bu, wd, g2, b2, *, alpha):
    rows, d_model = x.shape
    tile = TILE_FFN
    assert rows % tile == 0
    d_ff = wu.shape[1]
    return pl.pallas_call(
        functools.partial(_ffn_kernel, alpha=alpha, ff_chunk=min(2048, d_ff),
                          row_block=min(ROW_BLOCK, tile)),
        grid=(rows // tile,),
        in_specs=[pl.BlockSpec((tile, d_model), lambda i: (i, 0)),
                  pl.BlockSpec((tile, d_model), lambda i: (i, 0)),
                  _const_spec((1, d_model)), _const_spec((1, d_model)),
                  _const_spec((1, d_model)), _const_spec((1, d_model)),
                  _const_spec((d_model, d_ff)), _const_spec((1, d_ff)),
                  _const_spec((d_ff, d_model)), _const_spec((1, d_model)),
                  _const_spec((1, d_model))],
        out_specs=pl.BlockSpec((tile, d_model), lambda i: (i, 0)),
        out_shape=jax.ShapeDtypeStruct((rows, d_model), F32),
        compiler_params=pltpu.CompilerParams(
            dimension_semantics=("arbitrary",), vmem_limit_bytes=V7X_VMEM_LIMIT),
        name="ffn",
    )(x, pre, g0, b0, g1, b1, wu, bu, wd, g2, b2)


def kernel(x, meta_tokens, ln0_g, ln0_b, w_in, b_in, qk_conv_w, qk_conv_b, s5_lambda_re, s5_lambda_im, s5_log_dt, s5_b_re, s5_b_im, s5_c_re, s5_c_im, s5_d, s5_w_glu, m_norm_g, m_w_out, w_o, ln1_g, ln1_b, w_up, b_up, w_down, ln2_g, ln2_b):
    bsz, seq, d_model = x.shape
    depth = w_in.shape[0]
    assert depth == 1, "the meta-token prologue is written for a single layer"
    assert meta_tokens.shape == (N_META, d_model)
    assert seq % TILE_MIX == 0 and seq % S5_CHUNK == 0 and MLSTM_CHUNK >= S5_CHUNK >= N_META
    alpha = (2.0 * depth) ** 0.25
    d_s5 = d_model // 2
    d_qk = qk_conv_w.shape[-1] // 2
    d_v = m_norm_g.shape[-1]
    dk, dv = d_qk // M_HEADS, d_v // M_HEADS
    gate_lo = d_s5 + 2 * d_qk + 2 * d_v
    gate_hi = gate_lo + 2 * M_HEADS
    assert w_in.shape[-1] == gate_hi + 2 * d_model

    row = lambda a: a.reshape(1, -1).astype(F32)
    w = w_in[0]
    bias = b_in[0]
    qk_lo, qk_hi = d_s5, d_s5 + 2 * d_qk
    w_t = jnp.concatenate([w[:, :qk_lo], w[:, qk_hi:gate_lo], w[:, gate_hi:]], axis=1).T.astype(BF16)
    b_t = jnp.concatenate([bias[:qk_lo], bias[qk_hi:gate_lo], bias[gate_hi:]]).reshape(-1, 1).astype(F32)
    w_qk = w[:, qk_lo:qk_hi].astype(BF16)
    b_qk = row(bias[qk_lo:qk_hi])
    w_gate = w[:, gate_lo:gate_hi]
    wgc = jnp.pad(w_gate, ((0, 0), (0, LANES - 2 * M_HEADS))).astype(BF16)
    bgc = jnp.pad(bias[gate_lo:gate_hi], (0, LANES - 2 * M_HEADS)).reshape(1, LANES).astype(F32)
    wgr = w_gate.T.astype(BF16)
    bgr = bias[gate_lo:gate_hi].reshape(2 * M_HEADS, 1).astype(F32)
    g0, b0 = row(ln0_g), row(ln0_b)
    conv_w = qk_conv_w[0].astype(F32)
    conv_b = row(qk_conv_b[0])
    mngt = jnp.broadcast_to(m_norm_g[0].astype(F32)[:, None], (d_v, MLSTM_CHUNK))
    mixer_consts = (s5_w_glu[0].T.astype(BF16), mngt, m_w_out[0].T.astype(BF16),
                    w_o[0].astype(BF16))
    inproj_consts = (g0, b0, w_t, b_t, w_qk, b_qk, conv_w, conv_b, wgc, bgc, wgr, bgr)

    pad = MLSTM_CHUNK - N_META
    x_meta = jnp.concatenate([jnp.zeros((pad, d_model), x.dtype), meta_tokens.astype(x.dtype)])[None]
    ut_m, q_m, k_m, v_m, o_m, gs_m, gm_m, gcol_m, grow_m, carry_m = _inproj(
        x_meta, *inproj_consts, jnp.zeros((CARRY_ROWS, 2 * d_qk), F32), n_pad=pad)
    zero_state = (jnp.zeros((M_HEADS, dv, dk), F32), jnp.zeros((M_HEADS, 8, dk), F32),
                  jnp.zeros((8, dk), F32))
    _, c_m, n_m, m_m = _mixer(
        jnp.zeros((1, d_s5, MLSTM_CHUNK), BF16), q_m, k_m, v_m, o_m, gs_m, gm_m, gcol_m, grow_m,
        *mixer_consts, *zero_state, n_pad=pad, emit_state=True)

    ut, q, k, v, o, gs, gm, gcol, grow, _ = _inproj(x, *inproj_consts, carry_m[0], n_pad=0)
    s5_consts = _s5_weights(s5_lambda_re[0], s5_lambda_im[0], s5_log_dt[0], s5_b_re[0], s5_b_im[0],
                            s5_c_re[0], s5_c_im[0], s5_d[0])
    yst = _s5(ut, ut_m[0, :, MLSTM_CHUNK - S5_CHUNK:], *s5_consts)
    (pre,) = _mixer(yst, q, k, v, o, gs, gm, gcol, grow, *mixer_consts, c_m[0], n_m[0], m_m[0],
                    n_pad=0, emit_state=False)
    out = _ffn(x.reshape(bsz * seq, d_model), pre.reshape(bsz * seq, d_model), g0, b0,
               row(ln1_g[0]), row(ln1_b[0]), w_up[0].astype(BF16), row(b_up[0]),
               w_down[0].astype(BF16), row(ln2_g[0]), row(ln2_b[0]), alpha=alpha)
    return out.reshape(bsz, seq, d_model)
```

```python
import functools
import math

import jax
import jax.numpy as jnp
from jax import lax
from jax.experimental import pallas as pl
from jax.experimental.pallas import tpu as pltpu

F32 = jnp.float32
BF16 = jnp.bfloat16

N_META = 16
S5_GROUP = 16
S5_STATE = 64
M_HEADS = 4
CONV_WIDTH = 4
LN_EPS = 1e-5

LANES = 128
V7X_MXU_WIDTH = 256
S5_CHUNK = LANES
TILE_IN = 512
MLSTM_CHUNK = 256
TILE_MIX = 1024
TILE_FFN = 512
ROW_BLOCK = 256
CARRY_ROWS = 8
V7X_VMEM_LIMIT = 56 * 1024 * 1024

NEG_INF = float("-inf")


def _layer_norm(x, g, b):
    mu = jnp.mean(x, axis=-1, keepdims=True)
    xc = x - mu
    var = jnp.mean(xc * xc, axis=-1, keepdims=True)
    return xc * lax.rsqrt(var + LN_EPS) * g + b


def _sigmoid(x):
    return 0.5 * jnp.tanh(0.5 * x) + 0.5


def _log_sigmoid(x):
    return jnp.minimum(x, 0.0) - jnp.log1p(jnp.exp(-jnp.abs(x)))


def _gelu_tanh(x):
    c = math.sqrt(2.0 / math.pi)
    return x * (0.5 * (1.0 + jnp.tanh(c * (x + 0.044715 * (x * x * x)))))


def _dot(a, b):
    return jnp.dot(a, b, preferred_element_type=F32)


def _dot_nt(a, b):
    return lax.dot_general(a, b, (((1,), (1,)), ((), ())), preferred_element_type=F32)


def _dot_tn(a, b):
    return lax.dot_general(a, b, (((0,), (0,)), ((), ())), preferred_element_type=F32)


def _split3(x):
    hi = x.astype(BF16)
    r = x - hi.astype(F32)
    mid = r.astype(BF16)
    lo = (r - mid.astype(F32)).astype(BF16)
    return hi, mid, lo


def _const_spec(shape):
    nd = len(shape)
    return pl.BlockSpec(shape, lambda *_: (0,) * nd, pipeline_mode=pl.Buffered(1))


def _inproj_kernel(x_ref, g0_ref, b0_ref, wt_ref, bt_ref, w_ref, b_ref, cw_ref, cb_ref,
                   wgc_ref, bgc_ref, wgr_ref, bgr_ref, carry_in_ref,
                   ut_ref, q_ref, k_ref, vt_ref, ot_ref, gst_ref, gmt_ref, gcol_ref, grow_ref,
                   carry_out_ref,
                   ext_ref, *, tile, row_block, n_pad, d_qk, d_s5, d_v, d_model):
    j = pl.program_id(1)

    @pl.when(j == 0)
    def _():
        ext_ref[0:CARRY_ROWS, :] = carry_in_ref[...]

    blocks = [slice(r0, r0 + row_block) for r0 in range(0, tile, row_block)]

    def seg_t(lo, n_rows, rows, hb):
        return _dot_nt(wt_ref[lo:lo + n_rows, :], hb) + bt_ref[lo:lo + n_rows, :]

    hbs = []
    for rows in blocks:
        hb = _layer_norm(x_ref[rows, :], g0_ref[...], b0_ref[...]).astype(BF16)
        hbs.append(hb)
        qk = _dot(hb, w_ref[...]) + b_ref[...]
        if n_pad:
            t_idx = lax.broadcasted_iota(jnp.int32, (row_block, 1), 0) + (j * tile + rows.start)
            qk = jnp.where(t_idx >= n_pad, qk, 0.0)
        ext_ref[CARRY_ROWS + rows.start:CARRY_ROWS + rows.stop, :] = qk
        gcol_ref[rows, :] = _dot(hb, wgc_ref[...]) + bgc_ref[...]
        grow_ref[:, rows] = _dot_nt(wgr_ref[...], hb) + bgr_ref[...]
        ut = seg_t(0, d_s5, rows, hb)
        if n_pad:
            cols = lax.broadcasted_iota(jnp.int32, (1, row_block), 1) + (j * tile + rows.start)
            ut = jnp.where(cols >= n_pad, ut, 0.0)
        ut_ref[:, rows] = ut.astype(BF16)
        vt_ref[:, rows] = seg_t(d_s5, d_v, rows, hb).astype(BF16)
    off = d_s5 + d_v

    ext = ext_ref[...]
    conv = cb_ref[...] + cw_ref[CONV_WIDTH - 1:CONV_WIDTH, :] * ext[CARRY_ROWS:, :]
    for r in range(CONV_WIDTH - 1):
        lag = CONV_WIDTH - 1 - r
        conv = conv + cw_ref[r:r + 1, :] * pltpu.roll(ext, lag, 0)[CARRY_ROWS:, :]
    tail = ext[tile:, :]
    ext_ref[0:CARRY_ROWS, :] = tail
    carry_out_ref[...] = tail
    act = conv * _sigmoid(conv)
    q_ref[...] = act[:, :d_qk].astype(BF16)
    k_ref[...] = (act[:, d_qk:] * ((d_qk // M_HEADS) ** -0.5)).astype(BF16)

    for out_ref, n_rows in ((ot_ref, d_v), (gst_ref, d_model), (gmt_ref, d_model)):
        for rows, hb in zip(blocks, hbs):
            out_ref[:, rows] = _sigmoid(seg_t(off, n_rows, rows, hb)).astype(BF16)
        off += n_rows


def _inproj(x, g0, b0, w_t, b_t, w_qk, b_qk, conv_w, conv_b, wgc, bgc, wgr, bgr, carry_in,
            *, n_pad):
    bsz, length, d_model = x.shape
    tile = min(TILE_IN, length)
    assert length % tile == 0
    d_qk = conv_w.shape[1] // 2
    d_s5 = d_model // 2
    d_v = d_model
    n_t = w_t.shape[0]
    assert n_t == d_s5 + 2 * d_v + 2 * d_model and w_qk.shape[1] == 2 * d_qk
    grid = (bsz, length // tile)

    def tok(width):
        return pl.BlockSpec((None, tile, width), lambda b, j: (b, j, 0))

    def feat(rows):
        return pl.BlockSpec((None, rows, tile), lambda b, j: (b, 0, j))

    def feat_out(rows):
        return jax.ShapeDtypeStruct((bsz, rows, length), BF16)

    kern = functools.partial(_inproj_kernel, tile=tile, row_block=min(ROW_BLOCK, tile), n_pad=n_pad,
                             d_qk=d_qk, d_s5=d_s5, d_v=d_v, d_model=d_model)
    return pl.pallas_call(
        kern,
        grid=grid,
        in_specs=[tok(d_model), _const_spec((1, d_model)), _const_spec((1, d_model)),
                  _const_spec((n_t, d_model)), _const_spec((n_t, 1)),
                  _const_spec((d_model, 2 * d_qk)), _const_spec((1, 2 * d_qk)),
                  _const_spec((CONV_WIDTH, 2 * d_qk)), _const_spec((1, 2 * d_qk)),
                  _const_spec(wgc.shape), _const_spec(bgc.shape), _const_spec(wgr.shape),
                  _const_spec(bgr.shape), _const_spec((CARRY_ROWS, 2 * d_qk))],
        out_specs=[feat(d_s5), tok(d_qk), tok(d_qk), feat(d_v), feat(d_v), feat(d_model),
                   feat(d_model), tok(LANES), feat(2 * M_HEADS),
                   pl.BlockSpec((None, CARRY_ROWS, 2 * d_qk), lambda b, j: (b, 0, 0))],
        out_shape=[feat_out(d_s5),
                   jax.ShapeDtypeStruct((bsz, length, d_qk), BF16),
                   jax.ShapeDtypeStruct((bsz, length, d_qk), BF16),
                   feat_out(d_v), feat_out(d_v), feat_out(d_model), feat_out(d_model),
                   jax.ShapeDtypeStruct((bsz, length, LANES), F32),
                   jax.ShapeDtypeStruct((bsz, 2 * M_HEADS, length), F32),
                   jax.ShapeDtypeStruct((bsz, CARRY_ROWS, 2 * d_qk), F32)],
        scratch_shapes=[pltpu.VMEM((CARRY_ROWS + tile, 2 * d_qk), F32)],
        compiler_params=pltpu.CompilerParams(
            dimension_semantics=("arbitrary", "arbitrary"), vmem_limit_bytes=V7X_VMEM_LIMIT),
        name="inproj",
    )(x, g0, b0, w_t, b_t, w_qk, b_qk, conv_w, conv_b, wgc, bgc, wgr, bgr, carry_in)


def _s5_kernel(ut_ref, um_ref, taps_ref, wst_ref, wc_ref, a1_ref, a2_ref, dsk_ref, yt_ref,
               px_ref, lhs_ref, toe_ref, s_ref, xin_ref, y_ref, ymid_ref, *, n_chunks, bsz):
    t_len = S5_CHUNK
    n_ch = S5_GROUP
    length = n_chunks * t_len
    half = 2 * S5_STATE
    rows_bj = bsz * n_ch
    lane_step = 1024

    r_idx = lax.broadcasted_iota(jnp.int32, (rows_bj, rows_bj), 0)
    q_idx = lax.broadcasted_iota(jnp.int32, (rows_bj, rows_bj), 1)
    perm = jnp.where(q_idx == (r_idx % bsz) * n_ch + r_idx // bsz, 1.0, 0.0).astype(BF16)
    perm_back = jnp.where(q_idx == (r_idx % n_ch) * bsz + r_idx // n_ch, 1.0, 0.0).astype(BF16)
    x_bj = ut_ref[...].reshape(rows_bj, length)
    for lo in range(0, length, lane_step):
        px_ref[:, lo:lo + lane_step] = _dot(perm, x_bj[:, lo:lo + lane_step]).astype(BF16)

    for jj in range(n_ch):
        lhs_ref[0:bsz, jj * t_len:(jj + 1) * t_len] = jnp.broadcast_to(
            um_ref[jj:jj + 1, :], (bsz, t_len))
    for c in range(n_chunks):
        for jj in range(n_ch):
            lhs_ref[(c + 1) * bsz:(c + 2) * bsz, jj * t_len:(jj + 1) * t_len] = (
                px_ref[jj * bsz:(jj + 1) * bsz, c * t_len:(c + 1) * t_len])

    lhs = lhs_ref[...]
    wst = wst_ref[...].reshape(n_ch * t_len, 2 * half)
    s_ref[...] = _dot(lhs, wst)
    a1 = a1_ref[...]
    a2 = a2_ref[...]
    state = jnp.zeros((bsz, 2 * half), F32)
    for c in range(n_chunks + 1):
        xin_ref[c * bsz:(c + 1) * bsz, :] = state[:, :half]
        swapped = jnp.concatenate([state[:, half:], state[:, :half]], axis=1)
        state = a1 * state + a2 * swapped + s_ref[c * bsz:(c + 1) * bsz, :]
    xin = xin_ref[...].astype(BF16)

    s_io = lax.broadcasted_iota(jnp.int32, (t_len, t_len), 0)
    t_io = lax.broadcasted_iota(jnp.int32, (t_len, t_len), 1)
    lower = t_io >= s_io
    cols_per_step = V7X_MXU_WIDTH // t_len
    for i0 in range(0, n_ch, cols_per_step):
        cols = slice(i0 * t_len, (i0 + cols_per_step) * t_len)
        for ii in range(i0, i0 + cols_per_step):
            for jj in range(n_ch):
                tap = jnp.broadcast_to(taps_ref[jj, ii:ii + 1, :], (t_len, t_len))
                blk = pltpu.roll(tap, 0, 1, stride=1, stride_axis=0)
                toe_ref[jj * t_len:(jj + 1) * t_len, ii * t_len:(ii + 1) * t_len] = (
                    jnp.where(lower, blk, 0.0).astype(BF16))
        y_ref[:, cols] = (_dot(lhs, toe_ref[:, cols]) + _dot(xin, wc_ref[:, cols])
                          + lhs[:, cols].astype(F32) * dsk_ref[:, cols])

    for c in range(n_chunks):
        for ii in range(n_ch):
            ymid_ref[ii * bsz:(ii + 1) * bsz, c * t_len:(c + 1) * t_len] = (
                y_ref[(c + 1) * bsz:(c + 2) * bsz, ii * t_len:(ii + 1) * t_len].astype(BF16))
    for lo in range(0, length, lane_step):
        yt_ref[:, :, lo:lo + lane_step] = _dot(
            perm_back, ymid_ref[:, lo:lo + lane_step]).astype(BF16).reshape(bsz, n_ch, lane_step)


def _s5(ut, um, taps, wst, wc, a1, a2, dskip):
    bsz, d_s5, length = ut.shape
    groups = d_s5 // S5_GROUP
    t_len = S5_CHUNK
    n_chunks = length // t_len
    assert length % 1024 == 0
    width = S5_GROUP * t_len
    half = 2 * S5_STATE
    rows = (n_chunks + 1) * bsz

    def grp(*shape):
        return pl.BlockSpec((None,) + shape, lambda g: (g,) + (0,) * len(shape))

    return pl.pallas_call(
        functools.partial(_s5_kernel, n_chunks=n_chunks, bsz=bsz),
        grid=(groups,),
        in_specs=[pl.BlockSpec((bsz, S5_GROUP, length), lambda g: (0, g, 0)),
                  pl.BlockSpec((S5_GROUP, t_len), lambda g: (g, 0)),
                  grp(S5_GROUP, S5_GROUP, t_len), grp(S5_GROUP, t_len, 2 * half), grp(half, width),
                  grp(1, 2 * half), grp(1, 2 * half), grp(1, width)],
        out_specs=pl.BlockSpec((bsz, S5_GROUP, length), lambda g: (0, g, 0)),
        out_shape=jax.ShapeDtypeStruct((bsz, d_s5, length), BF16),
        scratch_shapes=[pltpu.VMEM((bsz * S5_GROUP, length), BF16),
                        pltpu.VMEM((rows, width), BF16),
                        pltpu.VMEM((width, width), BF16),
                        pltpu.VMEM((rows, 2 * half), F32),
                        pltpu.VMEM((rows, half), F32),
                        pltpu.VMEM((rows, width), F32),
                        pltpu.VMEM((bsz * S5_GROUP, length), BF16)],
        compiler_params=pltpu.CompilerParams(
            dimension_semantics=("arbitrary",), vmem_limit_bytes=V7X_VMEM_LIMIT),
        name="s5",
    )(ut, um, taps, wst, wc, a1, a2, dskip)


def _s5_weights(lam_re, lam_im, log_dt, b_re, b_im, c_re, c_im, d_skip):
    hp = lax.Precision.HIGHEST
    t_len = S5_CHUNK
    groups, n_state = lam_re.shape
    lr, li = lam_re.astype(F32), lam_im.astype(F32)
    dt = jnp.exp(log_dt.astype(F32))[:, None]
    ar, ai = lr * dt, li * dt
    steps = jnp.arange(t_len + 1, dtype=F32)[:, None, None]
    mag = jnp.exp(ar[None] * steps)
    pr, pi = mag * jnp.cos(ai[None] * steps), mag * jnp.sin(ai[None] * steps)
    nr, ni = pr[1] - 1.0, pi[1]
    den = lr * lr + li * li
    fr, fi = (nr * lr + ni * li) / den, (ni * lr - nr * li) / den
    bre, bim = b_re.astype(F32), b_im.astype(F32)
    bbr = fr[..., None] * bre - fi[..., None] * bim
    bbi = fr[..., None] * bim + fi[..., None] * bre
    cre, cim = c_re.astype(F32), c_im.astype(F32)
    cbr = cre[:, :, None, :] * jnp.transpose(bbr, (0, 2, 1))[:, None] \
        - cim[:, :, None, :] * jnp.transpose(bbi, (0, 2, 1))[:, None]
    cbi = cre[:, :, None, :] * jnp.transpose(bbi, (0, 2, 1))[:, None] \
        + cim[:, :, None, :] * jnp.transpose(bbr, (0, 2, 1))[:, None]
    taps = (jnp.einsum("gijp,dgp->gjid", cbr, pr[:t_len], precision=hp)
            - jnp.einsum("gijp,dgp->gjid", cbi, pi[:t_len], precision=hp))

    prg = jnp.transpose(pr[:t_len][::-1], (1, 0, 2))
    pig = jnp.transpose(pi[:t_len][::-1], (1, 0, 2))
    bjr = jnp.transpose(bbr, (0, 2, 1))
    bji = jnp.transpose(bbi, (0, 2, 1))
    p_re4 = jnp.tile(prg, (1, 1, 4))[:, None]
    p_im4 = jnp.tile(pig, (1, 1, 4))[:, None]
    b_u = jnp.concatenate([bjr, bji, bji, bjr], axis=-1)[:, :, None]
    b_v = jnp.concatenate([-bji, bjr, bjr, -bji], axis=-1)[:, :, None]
    wst = (p_re4 * b_u + p_im4 * b_v).astype(BF16)

    ptr = jnp.transpose(pr[1:], (1, 2, 0))
    pti = jnp.transpose(pi[1:], (1, 2, 0))
    ctr = jnp.transpose(cre, (0, 2, 1))
    cti = jnp.transpose(cim, (0, 2, 1))
    p_re2 = jnp.tile(ptr, (1, 2, 1))[:, :, None]
    p_im2 = jnp.tile(pti, (1, 2, 1))[:, :, None]
    c_u = jnp.concatenate([ctr, -cti], axis=1)[..., None]
    c_v = jnp.concatenate([-cti, -ctr], axis=1)[..., None]
    wc = (c_u * p_re2 + c_v * p_im2).astype(BF16).reshape(groups, 2 * n_state, S5_GROUP * t_len)

    a_re, a_im = pr[t_len], pi[t_len]
    a1 = jnp.concatenate([a_re, a_re, a_re, a_re], axis=-1)[:, None, :]
    a2 = jnp.concatenate([-a_im, a_im, a_im, -a_im], axis=-1)[:, None, :]
    dsk = jnp.repeat(d_skip.astype(F32).reshape(groups, 1, S5_GROUP), t_len, axis=-1)
    return taps, wst, wc, a1, a2, dsk


def _mixer_kernel(yst_ref, q_ref, k_ref, vt_ref, ot_ref, gst_ref, gmt_ref, gcol_ref, grow_ref,
                  wglut_ref, mngt_ref, wmot_ref, wo_ref, c0_ref, n0_ref, m0_ref,
                  *refs, tile, chunk, n_pad, emit_state, d_model):
    if emit_state:
        out_ref, c_out_ref, n_out_ref, m_out_ref, ct_ref, n_ref, m_ref, hmt_ref = refs
    else:
        out_ref, ct_ref, n_ref, m_ref, hmt_ref = refs
    j = pl.program_id(1)
    dk = q_ref.shape[-1] // M_HEADS
    dv = vt_ref.shape[0] // M_HEADS

    @pl.when(j == 0)
    def _():
        ct_ref[...] = c0_ref[...]
        n_ref[...] = n0_ref[...]
        m_ref[...] = m0_ref[...]

    heads = range(M_HEADS)
    chunks = [slice(c0, c0 + chunk) for c0 in range(0, tile, chunk)]
    ti = lax.broadcasted_iota(jnp.int32, (chunk, chunk), 0)
    tj = lax.broadcasted_iota(jnp.int32, (chunk, chunk), 1)
    tri_lower = jnp.where(ti >= tj, 1.0, 0.0).astype(BF16)
    tri_upper = jnp.where(ti <= tj, 1.0, 0.0).astype(BF16)

    qs = [[q_ref[cs, h * dk:(h + 1) * dk] for h in heads] for cs in chunks]
    ks = [[k_ref[cs, h * dk:(h + 1) * dk] for h in heads] for cs in chunks]
    vts = [[vt_ref[h * dv:(h + 1) * dv, cs] for h in heads] for cs in chunks]
    state = {"m": [m_ref[h:h + 1, 0:1] for h in heads],
             "ct": [ct_ref[h] for h in heads],
             "n": [n_ref[h] for h in heads]}
    kq, zt, m_sts, cq, nq, brs, r_cols = {}, {}, {}, {}, {}, {}, {}

    def stage1(c):
        cs = chunks[c]
        m_st, ct_st, n_st = state["m"], state["ct"], state["n"]
        kq[c] = [_dot_nt(ks[c][h], qs[c][h]) for h in heads]
        zt[c] = _dot(wglut_ref[...], _gelu_tanh(yst_ref[:, cs].astype(F32)).astype(BF16))
        g_col = gcol_ref[cs, :]
        g_row = grow_ref[:, cs]
        lane = lax.broadcasted_iota(jnp.int32, g_col.shape, 1)
        sub = lax.broadcasted_iota(jnp.int32, g_row.shape, 0)
        lf_col = jnp.where((lane >= M_HEADS) & (lane < 2 * M_HEADS), _log_sigmoid(g_col), 0.0)
        lf_row = jnp.where(sub >= M_HEADS, _log_sigmoid(g_row), 0.0)
        li_col, li_row = g_col, g_row
        if n_pad:
            t_col = lax.broadcasted_iota(jnp.int32, g_col.shape, 0) + (j * tile + cs.start)
            t_row = lax.broadcasted_iota(jnp.int32, g_row.shape, 1) + (j * tile + cs.start)
            lf_col = jnp.where(t_col >= n_pad, lf_col, 0.0)
            lf_row = jnp.where(t_row >= n_pad, lf_row, 0.0)
            li_col = jnp.where(t_col >= n_pad, li_col, NEG_INF)
            li_row = jnp.where(t_row >= n_pad, li_row, NEG_INF)
        b_col = sum(_dot(tri_lower, part) for part in _split3(lf_col))
        b_row = sum(_dot(part, tri_upper) for part in _split3(lf_row))

        brs[c] = [b_row[M_HEADS + h:M_HEADS + h + 1, :] for h in heads]
        r_cols[c] = [li_col[:, h:h + 1] - b_col[:, M_HEADS + h:M_HEADS + h + 1] for h in heads]
        r_rows = [li_row[h:h + 1, :] - brs[c][h] for h in heads]

        m_sts[c] = m_st
        cq[c] = [_dot_nt(ct_st[h].astype(BF16), qs[c][h]) for h in heads]
        nq[c] = [_dot_nt(n_st[h].astype(BF16), qs[c][h])[0:1, :] for h in heads]
        a_last = [jnp.maximum(m_st[h], jnp.max(r_rows[h], axis=-1, keepdims=True)) for h in heads]
        w_k = [jnp.exp(r_rows[h] - a_last[h]) for h in heads]
        upd = [_dot((vts[c][h].astype(F32) * w_k[h]).astype(BF16), ks[c][h]) for h in heads]
        n_upd = [_dot(jnp.broadcast_to(w_k[h], (8, chunk)).astype(BF16), ks[c][h]) for h in heads]
        decay = [jnp.exp(m_st[h] - a_last[h]) for h in heads]
        state["ct"] = [decay[h] * ct_st[h] + upd[h] for h in heads]
        state["n"] = [decay[h] * n_st[h] + n_upd[h] for h in heads]
        state["m"] = [brs[c][h][:, chunk - 1:chunk] + a_last[h] for h in heads]
        if c == len(chunks) - 1:
            for h in heads:
                ct_ref[h] = state["ct"][h]
                n_ref[h] = state["n"][h]
                m_ref[h:h + 1, :] = jnp.broadcast_to(state["m"][h], (1, m_ref.shape[1]))

    a_rows, w_inters, s_mats, vs_t, y_m = {}, {}, {}, {}, {}

    def stage2(c):
        e_mats = [jnp.where(ti <= tj, r_cols[c][h], NEG_INF) for h in heads]
        a_rows[c] = [jnp.maximum(m_sts[c][h], jnp.max(e_mats[h], axis=0, keepdims=True))
                     for h in heads]
        w_inters[c] = [jnp.exp(m_sts[c][h] - a_rows[c][h]) for h in heads]
        s_mats[c] = [kq[c][h] * jnp.exp(e_mats[h] - a_rows[c][h]) for h in heads]
        vs_t[c] = [_dot(vts[c][h], s_mats[c][h].astype(BF16)) for h in heads]

    def stage3(c):
        cs = chunks[c]
        for h in heads:
            den = jnp.sum(s_mats[c][h], axis=0, keepdims=True) + w_inters[c][h] * nq[c][h]
            scale = 1.0 / jnp.maximum(jnp.abs(den), jnp.exp(-(brs[c][h] + a_rows[c][h])))
            hh = (vs_t[c][h] + w_inters[c][h] * cq[c][h]) * scale
            hc = hh - jnp.mean(hh, axis=0, keepdims=True)
            var = jnp.mean(hc * hc, axis=0, keepdims=True)
            hn = hc * lax.rsqrt(var + LN_EPS) * mngt_ref[h * dv:(h + 1) * dv, :]
            hmt_ref[h * dv:(h + 1) * dv, cs] = (
                ot_ref[h * dv:(h + 1) * dv, cs].astype(F32) * hn).astype(BF16)
        y_m[c] = _dot(wmot_ref[...], hmt_ref[:, cs])

    def stage4(c):
        cs = chunks[c]
        y_s5 = zt[c][:d_model, :] * _sigmoid(zt[c][d_model:, :])
        mix = gst_ref[:, cs].astype(F32) * y_s5 + gmt_ref[:, cs].astype(F32) * y_m[c]
        out_ref[cs, :] = _dot_tn(mix.astype(BF16), wo_ref[...])

    stages = (stage1, stage2, stage3, stage4)
    for wave in range(len(chunks) + len(stages) - 1):
        for depth, stage in enumerate(stages):
            c = wave - depth
            if 0 <= c < len(chunks):
                stage(c)
    if emit_state:
        c_out_ref[...] = ct_ref[...]
        n_out_ref[...] = n_ref[...]
        m_out_ref[...] = m_ref[...]


def _mixer(yst, q, k, vt, ot, gst, gmt, gcol, grow, wglut, mngt, wmot, wo, c0, n0, m0,
           *, n_pad, emit_state):
    bsz, d_model, length = gst.shape
    tile = min(TILE_MIX, length)
    chunk = MLSTM_CHUNK
    assert length % tile == 0 and tile % chunk == 0 and mngt.shape[1] == chunk
    d_qk = q.shape[-1]
    d_v = vt.shape[1]
    d_s5 = yst.shape[1]
    dk, dv = d_qk // M_HEADS, d_v // M_HEADS
    grid = (bsz, length // tile)

    def tok(width):
        return pl.BlockSpec((None, tile, width), lambda b, j: (b, j, 0))

    def feat(rows):
        return pl.BlockSpec((None, rows, tile), lambda b, j: (b, 0, j))

    consts = [wglut, mngt, wmot, wo, c0, n0, m0]
    in_specs = ([feat(d_s5), tok(d_qk), tok(d_qk), feat(d_v), feat(d_v), feat(d_model),
                 feat(d_model), tok(LANES), feat(2 * M_HEADS)]
                + [_const_spec(c.shape) for c in consts])
    out_specs = [tok(d_model)]
    out_shape = [jax.ShapeDtypeStruct((bsz, length, d_model), F32)]
    if emit_state:
        out_specs += [pl.BlockSpec((None, M_HEADS, dv, dk), lambda b, j: (b, 0, 0, 0)),
                      pl.BlockSpec((None, M_HEADS, 8, dk), lambda b, j: (b, 0, 0, 0)),
                      pl.BlockSpec((None, 8, dk), lambda b, j: (b, 0, 0))]
        out_shape += [jax.ShapeDtypeStruct((bsz, M_HEADS, dv, dk), F32),
                      jax.ShapeDtypeStruct((bsz, M_HEADS, 8, dk), F32),
                      jax.ShapeDtypeStruct((bsz, 8, dk), F32)]
    kern = functools.partial(_mixer_kernel, tile=tile, chunk=chunk, n_pad=n_pad,
                             emit_state=emit_state, d_model=d_model)
    return pl.pallas_call(
        kern,
        grid=grid,
        in_specs=in_specs,
        out_specs=out_specs,
        out_shape=out_shape,
        scratch_shapes=[pltpu.VMEM((M_HEADS, dv, dk), F32), pltpu.VMEM((M_HEADS, 8, dk), F32),
                        pltpu.VMEM((8, dk), F32), pltpu.VMEM((d_v, tile), BF16)],
        compiler_params=pltpu.CompilerParams(
            dimension_semantics=("arbitrary", "arbitrary"), vmem_limit_bytes=V7X_VMEM_LIMIT),
        name="mixer",
    )(yst, q, k, vt, ot, gst, gmt, gcol, grow, *consts)


def _ffn_kernel(x_ref, pre_ref, g0_ref, b0_ref, g1_ref, b1_ref, wu_ref, bu_ref, wd_ref, g2_ref,
                b2_ref, out_ref, *, alpha, ff_chunk, row_block):
    d_ff = wu_ref.shape[1]
    tile = x_ref.shape[0]
    for r0 in range(0, tile, row_block):
        rows = slice(r0, r0 + row_block)
        h0 = _layer_norm(x_ref[rows, :], g0_ref[...], b0_ref[...])
        h = _layer_norm(alpha * h0 + pre_ref[rows, :], g1_ref[...], b1_ref[...])
        hb = h.astype(BF16)
        acc = alpha * h
        for c in range(d_ff // ff_chunk):
            lo = c * ff_chunk
            a = jnp.maximum(
                _dot(hb, wu_ref[:, lo:lo + ff_chunk]) + bu_ref[:, lo:lo + ff_chunk], 0.0)
            acc = acc + _dot((a * a).astype(BF16), wd_ref[lo:lo + ff_chunk, :])
        out_ref[rows, :] = _layer_norm(acc, g2_ref[...], b2_ref[...])


def _ffn(x, pre, g0, b0, g1, b1, wu, bu, wd, g2, b2, *, alpha):
    rows, d_model = x.shape
    tile = TILE_FFN
    assert rows % tile == 0
    d_ff = wu.shape[1]
    return pl.pallas_call(
        functools.partial(_ffn_kernel, alpha=alpha, ff_chunk=min(2048, d_ff),
                          row_block=min(ROW_BLOCK, tile)),
        grid=(rows // tile,),
        in_specs=[pl.BlockSpec((tile, d_model), lambda i: (i, 0)),
                  pl.BlockSpec((tile, d_model), lambda i: (i, 0)),
                  _const_spec((1, d_model)), _const_spec((1, d_model)),
                  _const_spec((1, d_model)), _const_spec((1, d_model)),
                  _const_spec((d_model, d_ff)), _const_spec((1, d_ff)),
                  _const_spec((d_ff, d_model)), _const_spec((1, d_model)),
                  _const_spec((1, d_model))],
        out_specs=pl.BlockSpec((tile, d_model), lambda i: (i, 0)),
        out_shape=jax.ShapeDtypeStruct((rows, d_model), F32),
        compiler_params=pltpu.CompilerParams(
            dimension_semantics=("arbitrary",), vmem_limit_bytes=V7X_VMEM_LIMIT),
        name="ffn",
    )(x, pre, g0, b0, g1, b1, wu, bu, wd, g2, b2)


def kernel(x, meta_tokens, ln0_g, ln0_b, w_in, b_in, qk_conv_w, qk_conv_b, s5_lambda_re, s5_lambda_im, s5_log_dt, s5_b_re, s5_b_im, s5_c_re, s5_c_im, s5_d, s5_w_glu, m_norm_g, m_w_out, w_o, ln1_g, ln1_b, w_up, b_up, w_down, ln2_g, ln2_b):
    bsz, seq, d_model = x.shape
    depth = w_in.shape[0]
    assert depth == 1, "the meta-token prologue is written for a single layer"
    assert meta_tokens.shape == (N_META, d_model)
    assert seq % TILE_MIX == 0 and seq % S5_CHUNK == 0 and MLSTM_CHUNK >= S5_CHUNK >= N_META
    alpha = (2.0 * depth) ** 0.25
    d_s5 = d_model // 2
    d_qk = qk_conv_w.shape[-1] // 2
    d_v = m_norm_g.shape[-1]
    dk, dv = d_qk // M_HEADS, d_v // M_HEADS
    gate_lo = d_s5 + 2 * d_qk + 2 * d_v
    gate_hi = gate_lo + 2 * M_HEADS
    assert w_in.shape[-1] == gate_hi + 2 * d_model

    row = lambda a: a.reshape(1, -1).astype(F32)
    w = w_in[0]
    bias = b_in[0]
    qk_lo, qk_hi = d_s5, d_s5 + 2 * d_qk
    w_t = jnp.concatenate([w[:, :qk_lo], w[:, qk_hi:gate_lo], w[:, gate_hi:]], axis=1).T.astype(BF16)
    b_t = jnp.concatenate([bias[:qk_lo], bias[qk_hi:gate_lo], bias[gate_hi:]]).reshape(-1, 1).astype(F32)
    w_qk = w[:, qk_lo:qk_hi].astype(BF16)
    b_qk = row(bias[qk_lo:qk_hi])
    w_gate = w[:, gate_lo:gate_hi]
    wgc = jnp.pad(w_gate, ((0, 0), (0, LANES - 2 * M_HEADS))).astype(BF16)
    bgc = jnp.pad(bias[gate_lo:gate_hi], (0, LANES - 2 * M_HEADS)).reshape(1, LANES).astype(F32)
    wgr = w_gate.T.astype(BF16)
    bgr = bias[gate_lo:gate_hi].reshape(2 * M_HEADS, 1).astype(F32)
    g0, b0 = row(ln0_g), row(ln0_b)
    conv_w = qk_conv_w[0].astype(F32)
    conv_b = row(qk_conv_b[0])
    mngt = jnp.broadcast_to(m_norm_g[0].astype(F32)[:, None], (d_v, MLSTM_CHUNK))
    mixer_consts = (s5_w_glu[0].T.astype(BF16), mngt, m_w_out[0].T.astype(BF16),
                    w_o[0].astype(BF16))
    inproj_consts = (g0, b0, w_t, b_t, w_qk, b_qk, conv_w, conv_b, wgc, bgc, wgr, bgr)

    pad = MLSTM_CHUNK - N_META
    x_meta = jnp.concatenate([jnp.zeros((pad, d_model), x.dtype), meta_tokens.astype(x.dtype)])[None]
    ut_m, q_m, k_m, v_m, o_m, gs_m, gm_m, gcol_m, grow_m, carry_m = _inproj(
        x_meta, *inproj_consts, jnp.zeros((CARRY_ROWS, 2 * d_qk), F32), n_pad=pad)
    zero_state = (jnp.zeros((M_HEADS, dv, dk), F32), jnp.zeros((M_HEADS, 8, dk), F32),
                  jnp.zeros((8, dk), F32))
    _, c_m, n_m, m_m = _mixer(
        jnp.zeros((1, d_s5, MLSTM_CHUNK), BF16), q_m, k_m, v_m, o_m, gs_m, gm_m, gcol_m, grow_m,
        *mixer_consts, *zero_state, n_pad=pad, emit_state=True)

    ut, q, k, v, o, gs, gm, gcol, grow, _ = _inproj(x, *inproj_consts, carry_m[0], n_pad=0)
    s5_consts = _s5_weights(s5_lambda_re[0], s5_lambda_im[0], s5_log_dt[0], s5_b_re[0], s5_b_im[0],
                            s5_c_re[0], s5_c_im[0], s5_d[0])
    yst = _s5(ut, ut_m[0, :, MLSTM_CHUNK - S5_CHUNK:], *s5_consts)
    (pre,) = _mixer(yst, q, k, v, o, gs, gm, gcol, grow, *mixer_consts, c_m[0], n_m[0], m_m[0],
                    n_pad=0, emit_state=False)
    out = _ffn(x.reshape(bsz * seq, d_model), pre.reshape(bsz * seq, d_model), g0, b0,
               row(ln1_g[0]), row(ln1_b[0]), w_up[0].astype(BF16), row(b_up[0]),
               w_down[0].astype(BF16), row(ln2_g[0]), row(ln2_b[0]), alpha=alpha)
    return out.reshape(bsz, seq, d_model)
```

```python
import functools
import math

import jax
import jax.numpy as jnp
from jax import lax
from jax.experimental import pallas as pl
from jax.experimental.pallas import tpu as pltpu

F32 = jnp.float32
BF16 = jnp.bfloat16

N_META = 16
S5_GROUP = 16
S5_STATE = 64
M_HEADS = 4
CONV_WIDTH = 4
LN_EPS = 1e-5

LANES = 128
V7X_MXU_WIDTH = 256
S5_CHUNK = LANES
TILE_IN = 512
MLSTM_CHUNK = 256
TILE_MIX = 1024
TILE_FFN = 512
ROW_BLOCK = 256
CARRY_ROWS = 8
V7X_VMEM_LIMIT = 56 * 1024 * 1024

NEG_INF = float("-inf")


def _layer_norm(x, g, b):
    mu = jnp.mean(x, axis=-1, keepdims=True)
    xc = x - mu
    var = jnp.mean(xc * xc, axis=-1, keepdims=True)
    return xc * lax.rsqrt(var + LN_EPS) * g + b


def _sigmoid(x):
    return 0.5 * jnp.tanh(0.5 * x) + 0.5


def _log_sigmoid(x):
    return jnp.minimum(x, 0.0) - jnp.log1p(jnp.exp(-jnp.abs(x)))


def _gelu_tanh(x):
    c = math.sqrt(2.0 / math.pi)
    return x * (0.5 * (1.0 + jnp.tanh(c * (x + 0.044715 * (x * x * x)))))


def _dot(a, b):
    return jnp.dot(a, b, preferred_element_type=F32)


def _dot_nt(a, b):
    return lax.dot_general(a, b, (((1,), (1,)), ((), ())), preferred_element_type=F32)


def _dot_tn(a, b):
    return lax.dot_general(a, b, (((0,), (0,)), ((), ())), preferred_element_type=F32)


def _split3(x):
    hi = x.astype(BF16)
    r = x - hi.astype(F32)
    mid = r.astype(BF16)
    lo = (r - mid.astype(F32)).astype(BF16)
    return hi, mid, lo


def _const_spec(shape):
    nd = len(shape)
    return pl.BlockSpec(shape, lambda *_: (0,) * nd, pipeline_mode=pl.Buffered(1))


def _inproj_kernel(x_ref, g0_ref, b0_ref, wt_ref, bt_ref, w_ref, b_ref, cw_ref, cb_ref,
                   wgc_ref, bgc_ref, wgr_ref, bgr_ref, carry_in_ref,
                   ut_ref, q_ref, k_ref, vt_ref, ot_ref, gst_ref, gmt_ref, gcol_ref, grow_ref,
                   carry_out_ref,
                   ext_ref, *, tile, row_block, n_pad, d_qk, d_s5, d_v, d_model):
    j = pl.program_id(1)

    @pl.when(j == 0)
    def _():
        ext_ref[0:CARRY_ROWS, :] = carry_in_ref[...]

    blocks = [slice(r0, r0 + row_block) for r0 in range(0, tile, row_block)]

    def seg_t(lo, n_rows, rows, hb):
        return _dot_nt(wt_ref[lo:lo + n_rows, :], hb) + bt_ref[lo:lo + n_rows, :]

    hbs = []
    for rows in blocks:
        hb = _layer_norm(x_ref[rows, :], g0_ref[...], b0_ref[...]).astype(BF16)
        hbs.append(hb)
        qk = _dot(hb, w_ref[...]) + b_ref[...]
        if n_pad:
            t_idx = lax.broadcasted_iota(jnp.int32, (row_block, 1), 0) + (j * tile + rows.start)
            qk = jnp.where(t_idx >= n_pad, qk, 0.0)
        ext_ref[CARRY_ROWS + rows.start:CARRY_ROWS + rows.stop, :] = qk
        gcol_ref[rows, :] = _dot(hb, wgc_ref[...]) + bgc_ref[...]
        grow_ref[:, rows] = _dot_nt(wgr_ref[...], hb) + bgr_ref[...]
        ut = seg_t(0, d_s5, rows, hb)
        if n_pad:
            cols = lax.broadcasted_iota(jnp.int32, (1, row_block), 1) + (j * tile + rows.start)
            ut = jnp.where(cols >= n_pad, ut, 0.0)
        ut_ref[:, rows] = ut.astype(BF16)
        vt_ref[:, rows] = seg_t(d_s5, d_v, rows, hb).astype(BF16)
    off = d_s5 + d_v

    ext = ext_ref[...]
    conv = cb_ref[...] + cw_ref[CONV_WIDTH - 1:CONV_WIDTH, :] * ext[CARRY_ROWS:, :]
    for r in range(CONV_WIDTH - 1):
        lag = CONV_WIDTH - 1 - r
        conv = conv + cw_ref[r:r + 1, :] * pltpu.roll(ext, lag, 0)[CARRY_ROWS:, :]
    tail = ext[tile:, :]
    ext_ref[0:CARRY_ROWS, :] = tail
    carry_out_ref[...] = tail
    act = conv * _sigmoid(conv)
    q_ref[...] = act[:, :d_qk].astype(BF16)
    k_ref[...] = (act[:, d_qk:] * ((d_qk // M_HEADS) ** -0.5)).astype(BF16)

    for out_ref, n_rows in ((ot_ref, d_v), (gst_ref, d_model), (gmt_ref, d_model)):
        for rows, hb in zip(blocks, hbs):
            out_ref[:, rows] = _sigmoid(seg_t(off, n_rows, rows, hb)).astype(BF16)
        off += n_rows


def _inproj(x, g0, b0, w_t, b_t, w_qk, b_qk, conv_w, conv_b, wgc, bgc, wgr, bgr, carry_in,
            *, n_pad):
    bsz, length, d_model = x.shape
    tile = min(TILE_IN, length)
    assert length % tile == 0
    d_qk = conv_w.shape[1] // 2
    d_s5 = d_model // 2
    d_v = d_model
    n_t = w_t.shape[0]
    assert n_t == d_s5 + 2 * d_v + 2 * d_model and w_qk.shape[1] == 2 * d_qk
    grid = (bsz, length // tile)

    def tok(width):
        return pl.BlockSpec((None, tile, width), lambda b, j: (b, j, 0))

    def feat(rows):
        return pl.BlockSpec((None, rows, tile), lambda b, j: (b, 0, j))

    def feat_out(rows):
        return jax.ShapeDtypeStruct((bsz, rows, length), BF16)

    kern = functools.partial(_inproj_kernel, tile=tile, row_block=min(ROW_BLOCK, tile), n_pad=n_pad,
                             d_qk=d_qk, d_s5=d_s5, d_v=d_v, d_model=d_model)
    return pl.pallas_call(
        kern,
        grid=grid,
        in_specs=[tok(d_model), _const_spec((1, d_model)), _const_spec((1, d_model)),
                  _const_spec((n_t, d_model)), _const_spec((n_t, 1)),
                  _const_spec((d_model, 2 * d_qk)), _const_spec((1, 2 * d_qk)),
                  _const_spec((CONV_WIDTH, 2 * d_qk)), _const_spec((1, 2 * d_qk)),
                  _const_spec(wgc.shape), _const_spec(bgc.shape), _const_spec(wgr.shape),
                  _const_spec(bgr.shape), _const_spec((CARRY_ROWS, 2 * d_qk))],
        out_specs=[feat(d_s5), tok(d_qk), tok(d_qk), feat(d_v), feat(d_v), feat(d_model),
                   feat(d_model), tok(LANES), feat(2 * M_HEADS),
                   pl.BlockSpec((None, CARRY_ROWS, 2 * d_qk), lambda b, j: (b, 0, 0))],
        out_shape=[feat_out(d_s5),
                   jax.ShapeDtypeStruct((bsz, length, d_qk), BF16),
                   jax.ShapeDtypeStruct((bsz, length, d_qk), BF16),
                   feat_out(d_v), feat_out(d_v), feat_out(d_model), feat_out(d_model),
                   jax.ShapeDtypeStruct((bsz, length, LANES), F32),
                   jax.ShapeDtypeStruct((bsz, 2 * M_HEADS, length), F32),
                   jax.ShapeDtypeStruct((bsz, CARRY_ROWS, 2 * d_qk), F32)],
        scratch_shapes=[pltpu.VMEM((CARRY_ROWS + tile, 2 * d_qk), F32)],
        compiler_params=pltpu.CompilerParams(
            dimension_semantics=("arbitrary", "arbitrary"), vmem_limit_bytes=V7X_VMEM_LIMIT),
        name="inproj",
    )(x, g0, b0, w_t, b_t, w_qk, b_qk, conv_w, conv_b, wgc, bgc, wgr, bgr, carry_in)


def _s5_kernel(ut_ref, um_ref, taps_ref, wst_ref, wc_ref, a1_ref, a2_ref, dsk_ref, yt_ref,
               px_ref, lhs_ref, toe_ref, s_ref, xin_ref, y_ref, ymid_ref, *, n_chunks, bsz):
    t_len = S5_CHUNK
    n_ch = S5_GROUP
    length = n_chunks * t_len
    half = 2 * S5_STATE
    rows_bj = bsz * n_ch
    lane_step = 1024

    r_idx = lax.broadcasted_iota(jnp.int32, (rows_bj, rows_bj), 0)
    q_idx = lax.broadcasted_iota(jnp.int32, (rows_bj, rows_bj), 1)
    perm = jnp.where(q_idx == (r_idx % bsz) * n_ch + r_idx // bsz, 1.0, 0.0).astype(BF16)
    perm_back = jnp.where(q_idx == (r_idx % n_ch) * bsz + r_idx // n_ch, 1.0, 0.0).astype(BF16)
    x_bj = ut_ref[...].reshape(rows_bj, length)
    for lo in range(0, length, lane_step):
        px_ref[:, lo:lo + lane_step] = _dot(perm, x_bj[:, lo:lo + lane_step]).astype(BF16)

    for jj in range(n_ch):
        lhs_ref[0:bsz, jj * t_len:(jj + 1) * t_len] = jnp.broadcast_to(
            um_ref[jj:jj + 1, :], (bsz, t_len))
    for c in range(n_chunks):
        for jj in range(n_ch):
            lhs_ref[(c + 1) * bsz:(c + 2) * bsz, jj * t_len:(jj + 1) * t_len] = (
                px_ref[jj * bsz:(jj + 1) * bsz, c * t_len:(c + 1) * t_len])

    lhs = lhs_ref[...]
    wst = wst_ref[...].reshape(n_ch * t_len, 2 * half)
    s_ref[...] = _dot(lhs, wst)
    a1 = a1_ref[...]
    a2 = a2_ref[...]
    state = jnp.zeros((bsz, 2 * half), F32)
    for c in range(n_chunks + 1):
        xin_ref[c * bsz:(c + 1) * bsz, :] = state[:, :half]
        swapped = jnp.concatenate([state[:, half:], state[:, :half]], axis=1)
        state = a1 * state + a2 * swapped + s_ref[c * bsz:(c + 1) * bsz, :]
    xin = xin_ref[...].astype(BF16)

    s_io = lax.broadcasted_iota(jnp.int32, (t_len, t_len), 0)
    t_io = lax.broadcasted_iota(jnp.int32, (t_len, t_len), 1)
    lower = t_io >= s_io
    cols_per_step = V7X_MXU_WIDTH // t_len
    for i0 in range(0, n_ch, cols_per_step):
        cols = slice(i0 * t_len, (i0 + cols_per_step) * t_len)
        for ii in range(i0, i0 + cols_per_step):
            for jj in range(n_ch):
                tap = jnp.broadcast_to(taps_ref[jj, ii:ii + 1, :], (t_len, t_len))
                blk = pltpu.roll(tap, 0, 1, stride=1, stride_axis=0)
                toe_ref[jj * t_len:(jj + 1) * t_len, ii * t_len:(ii + 1) * t_len] = (
                    jnp.where(lower, blk, 0.0).astype(BF16))
        wc_cols = jnp.concatenate([wc_ref[ii] for ii in range(i0, i0 + cols_per_step)], axis=1)
        y_ref[:, cols] = (_dot(lhs, toe_ref[:, cols]) + _dot(xin, wc_cols)
                          + lhs[:, cols].astype(F32) * dsk_ref[:, cols])

    for c in range(n_chunks):
        for ii in range(n_ch):
            ymid_ref[ii * bsz:(ii + 1) * bsz, c * t_len:(c + 1) * t_len] = (
                y_ref[(c + 1) * bsz:(c + 2) * bsz, ii * t_len:(ii + 1) * t_len].astype(BF16))
    for lo in range(0, length, lane_step):
        yt_ref[:, :, lo:lo + lane_step] = _dot(
            perm_back, ymid_ref[:, lo:lo + lane_step]).astype(BF16).reshape(bsz, n_ch, lane_step)


def _s5(ut, um, taps, wst, wc, a1, a2, dskip):
    bsz, d_s5, length = ut.shape
    groups = d_s5 // S5_GROUP
    t_len = S5_CHUNK
    n_chunks = length // t_len
    assert length % 1024 == 0
    width = S5_GROUP * t_len
    half = 2 * S5_STATE
    rows = (n_chunks + 1) * bsz

    def grp(*shape):
        return pl.BlockSpec((None,) + shape, lambda g: (g,) + (0,) * len(shape))

    return pl.pallas_call(
        functools.partial(_s5_kernel, n_chunks=n_chunks, bsz=bsz),
        grid=(groups,),
        in_specs=[pl.BlockSpec((bsz, S5_GROUP, length), lambda g: (0, g, 0)),
                  pl.BlockSpec((S5_GROUP, t_len), lambda g: (g, 0)),
                  grp(S5_GROUP, S5_GROUP, t_len), grp(S5_GROUP, t_len, 2 * half), grp(S5_GROUP, half, t_len),
                  grp(1, 2 * half), grp(1, 2 * half), grp(1, width)],
        out_specs=pl.BlockSpec((bsz, S5_GROUP, length), lambda g: (0, g, 0)),
        out_shape=jax.ShapeDtypeStruct((bsz, d_s5, length), BF16),
        scratch_shapes=[pltpu.VMEM((bsz * S5_GROUP, length), BF16),
                        pltpu.VMEM((rows, width), BF16),
                        pltpu.VMEM((width, width), BF16),
                        pltpu.VMEM((rows, 2 * half), F32),
                        pltpu.VMEM((rows, half), F32),
                        pltpu.VMEM((rows, width), F32),
                        pltpu.VMEM((bsz * S5_GROUP, length), BF16)],
        compiler_params=pltpu.CompilerParams(
            dimension_semantics=("arbitrary",), vmem_limit_bytes=V7X_VMEM_LIMIT),
        name="s5",
    )(ut, um, taps, wst, wc, a1, a2, dskip)


def _s5_weights(lam_re, lam_im, log_dt, b_re, b_im, c_re, c_im, d_skip):
    hp = lax.Precision.HIGHEST
    t_len = S5_CHUNK
    groups, n_state = lam_re.shape
    lr, li = lam_re.astype(F32), lam_im.astype(F32)
    dt = jnp.exp(log_dt.astype(F32))[:, None]
    ar, ai = lr * dt, li * dt
    steps = jnp.arange(t_len + 1, dtype=F32)[:, None, None]
    mag = jnp.exp(ar[None] * steps)
    pr, pi = mag * jnp.cos(ai[None] * steps), mag * jnp.sin(ai[None] * steps)
    nr, ni = pr[1] - 1.0, pi[1]
    den = lr * lr + li * li
    fr, fi = (nr * lr + ni * li) / den, (ni * lr - nr * li) / den
    bre, bim = b_re.astype(F32), b_im.astype(F32)
    bbr = fr[..., None] * bre - fi[..., None] * bim
    bbi = fr[..., None] * bim + fi[..., None] * bre
    cre, cim = c_re.astype(F32), c_im.astype(F32)
    cbr = cre[:, :, None, :] * jnp.transpose(bbr, (0, 2, 1))[:, None] \
        - cim[:, :, None, :] * jnp.transpose(bbi, (0, 2, 1))[:, None]
    cbi = cre[:, :, None, :] * jnp.transpose(bbi, (0, 2, 1))[:, None] \
        + cim[:, :, None, :] * jnp.transpose(bbr, (0, 2, 1))[:, None]
    taps = (jnp.einsum("gijp,dgp->gjid", cbr, pr[:t_len], precision=hp)
            - jnp.einsum("gijp,dgp->gjid", cbi, pi[:t_len], precision=hp))

    prg = jnp.transpose(pr[:t_len][::-1], (1, 0, 2))
    pig = jnp.transpose(pi[:t_len][::-1], (1, 0, 2))
    bjr = jnp.transpose(bbr, (0, 2, 1))
    bji = jnp.transpose(bbi, (0, 2, 1))
    p_re4 = jnp.tile(prg, (1, 1, 4))[:, None]
    p_im4 = jnp.tile(pig, (1, 1, 4))[:, None]
    b_u = jnp.concatenate([bjr, bji, bji, bjr], axis=-1)[:, :, None]
    b_v = jnp.concatenate([-bji, bjr, bjr, -bji], axis=-1)[:, :, None]
    wst = (p_re4 * b_u + p_im4 * b_v).astype(BF16)

    ptr = jnp.transpose(pr[1:], (1, 2, 0))
    pti = jnp.transpose(pi[1:], (1, 2, 0))
    ctr = jnp.transpose(cre, (0, 2, 1))
    cti = jnp.transpose(cim, (0, 2, 1))
    p_re2 = jnp.tile(ptr, (1, 2, 1))[:, None]
    p_im2 = jnp.tile(pti, (1, 2, 1))[:, None]
    c_u = jnp.transpose(jnp.concatenate([ctr, -cti], axis=1), (0, 2, 1))[..., None]
    c_v = jnp.transpose(jnp.concatenate([-cti, -ctr], axis=1), (0, 2, 1))[..., None]
    wc = (c_u * p_re2 + c_v * p_im2).astype(BF16)

    a_re, a_im = pr[t_len], pi[t_len]
    a1 = jnp.concatenate([a_re, a_re, a_re, a_re], axis=-1)[:, None, :]
    a2 = jnp.concatenate([-a_im, a_im, a_im, -a_im], axis=-1)[:, None, :]
    dsk = jnp.repeat(d_skip.astype(F32).reshape(groups, 1, S5_GROUP), t_len, axis=-1)
    return taps, wst, wc, a1, a2, dsk


def _mixer_kernel(yst_ref, q_ref, k_ref, vt_ref, ot_ref, gst_ref, gmt_ref, gcol_ref, grow_ref,
                  wglut_ref, mngt_ref, wmot_ref, wo_ref, c0_ref, n0_ref, m0_ref,
                  *refs, tile, chunk, n_pad, emit_state, d_model):
    if emit_state:
        out_ref, c_out_ref, n_out_ref, m_out_ref, ct_ref, n_ref, m_ref, hmt_ref = refs
    else:
        out_ref, ct_ref, n_ref, m_ref, hmt_ref = refs
    j = pl.program_id(1)
    dk = q_ref.shape[-1] // M_HEADS
    dv = vt_ref.shape[0] // M_HEADS

    @pl.when(j == 0)
    def _():
        ct_ref[...] = c0_ref[...]
        n_ref[...] = n0_ref[...]
        m_ref[...] = m0_ref[...]

    heads = range(M_HEADS)
    chunks = [slice(c0, c0 + chunk) for c0 in range(0, tile, chunk)]
    ti = lax.broadcasted_iota(jnp.int32, (chunk, chunk), 0)
    tj = lax.broadcasted_iota(jnp.int32, (chunk, chunk), 1)
    tri_lower = jnp.where(ti >= tj, 1.0, 0.0).astype(BF16)
    tri_upper = jnp.where(ti <= tj, 1.0, 0.0).astype(BF16)

    qs = [[q_ref[cs, h * dk:(h + 1) * dk] for h in heads] for cs in chunks]
    ks = [[k_ref[cs, h * dk:(h + 1) * dk] for h in heads] for cs in chunks]
    vts = [[vt_ref[h * dv:(h + 1) * dv, cs] for h in heads] for cs in chunks]
    state = {"m": [m_ref[h:h + 1, 0:1] for h in heads],
             "ct": [ct_ref[h] for h in heads],
             "n": [n_ref[h] for h in heads]}
    kq, zt, m_sts, cq, nq, brs, r_cols = {}, {}, {}, {}, {}, {}, {}

    def stage1(c):
        cs = chunks[c]
        m_st, ct_st, n_st = state["m"], state["ct"], state["n"]
        kq[c] = [_dot_nt(ks[c][h], qs[c][h]) for h in heads]
        zt[c] = _dot(wglut_ref[...], _gelu_tanh(yst_ref[:, cs].astype(F32)).astype(BF16))
        g_col = gcol_ref[cs, :]
        g_row = grow_ref[:, cs]
        lane = lax.broadcasted_iota(jnp.int32, g_col.shape, 1)
        sub = lax.broadcasted_iota(jnp.int32, g_row.shape, 0)
        lf_col = jnp.where((lane >= M_HEADS) & (lane < 2 * M_HEADS), _log_sigmoid(g_col), 0.0)
        lf_row = jnp.where(sub >= M_HEADS, _log_sigmoid(g_row), 0.0)
        li_col, li_row = g_col, g_row
        if n_pad:
            t_col = lax.broadcasted_iota(jnp.int32, g_col.shape, 0) + (j * tile + cs.start)
            t_row = lax.broadcasted_iota(jnp.int32, g_row.shape, 1) + (j * tile + cs.start)
            lf_col = jnp.where(t_col >= n_pad, lf_col, 0.0)
            lf_row = jnp.where(t_row >= n_pad, lf_row, 0.0)
            li_col = jnp.where(t_col >= n_pad, li_col, NEG_INF)
            li_row = jnp.where(t_row >= n_pad, li_row, NEG_INF)
        b_col = sum(_dot(tri_lower, part) for part in _split3(lf_col))
        b_row = sum(_dot(part, tri_upper) for part in _split3(lf_row))

        brs[c] = [b_row[M_HEADS + h:M_HEADS + h + 1, :] for h in heads]
        r_cols[c] = [li_col[:, h:h + 1] - b_col[:, M_HEADS + h:M_HEADS + h + 1] for h in heads]
        r_rows = [li_row[h:h + 1, :] - brs[c][h] for h in heads]

        m_sts[c] = m_st
        cq[c] = [_dot_nt(ct_st[h].astype(BF16), qs[c][h]) for h in heads]
        nq[c] = [_dot_nt(n_st[h].astype(BF16), qs[c][h])[0:1, :] for h in heads]
        a_last = [jnp.maximum(m_st[h], jnp.max(r_rows[h], axis=-1, keepdims=True)) for h in heads]
        w_k = [jnp.exp(r_rows[h] - a_last[h]) for h in heads]
        upd = [_dot((vts[c][h].astype(F32) * w_k[h]).astype(BF16), ks[c][h]) for h in heads]
        n_upd = [_dot(jnp.broadcast_to(w_k[h], (8, chunk)).astype(BF16), ks[c][h]) for h in heads]
        decay = [jnp.exp(m_st[h] - a_last[h]) for h in heads]
        state["ct"] = [decay[h] * ct_st[h] + upd[h] for h in heads]
        state["n"] = [decay[h] * n_st[h] + n_upd[h] for h in heads]
        state["m"] = [brs[c][h][:, chunk - 1:chunk] + a_last[h] for h in heads]
        if c == len(chunks) - 1:
            for h in heads:
                ct_ref[h] = state["ct"][h]
                n_ref[h] = state["n"][h]
                m_ref[h:h + 1, :] = jnp.broadcast_to(state["m"][h], (1, m_ref.shape[1]))

    a_rows, w_inters, s_mats, vs_t, y_m = {}, {}, {}, {}, {}

    def stage2(c):
        e_mats = [jnp.where(ti <= tj, r_cols[c][h], NEG_INF) for h in heads]
        a_rows[c] = [jnp.maximum(m_sts[c][h], jnp.max(e_mats[h], axis=0, keepdims=True))
                     for h in heads]
        w_inters[c] = [jnp.exp(m_sts[c][h] - a_rows[c][h]) for h in heads]
        s_mats[c] = [kq[c][h] * jnp.exp(e_mats[h] - a_rows[c][h]) for h in heads]
        vs_t[c] = [_dot(vts[c][h], s_mats[c][h].astype(BF16)) for h in heads]

    def stage3(c):
        cs = chunks[c]
        for h in heads:
            den = jnp.sum(s_mats[c][h], axis=0, keepdims=True) + w_inters[c][h] * nq[c][h]
            scale = 1.0 / jnp.maximum(jnp.abs(den), jnp.exp(-(brs[c][h] + a_rows[c][h])))
            hh = (vs_t[c][h] + w_inters[c][h] * cq[c][h]) * scale
            hc = hh - jnp.mean(hh, axis=0, keepdims=True)
            var = jnp.mean(hc * hc, axis=0, keepdims=True)
            hn = hc * lax.rsqrt(var + LN_EPS) * mngt_ref[h * dv:(h + 1) * dv, :]
            hmt_ref[h * dv:(h + 1) * dv, cs] = (
                ot_ref[h * dv:(h + 1) * dv, cs].astype(F32) * hn).astype(BF16)
        y_m[c] = _dot(wmot_ref[...], hmt_ref[:, cs])

    def stage4(c):
        cs = chunks[c]
        y_s5 = zt[c][:d_model, :] * _sigmoid(zt[c][d_model:, :])
        mix = gst_ref[:, cs].astype(F32) * y_s5 + gmt_ref[:, cs].astype(F32) * y_m[c]
        out_ref[cs, :] = _dot_tn(mix.astype(BF16), wo_ref[...])

    stages = (stage1, stage2, stage3, stage4)
    for wave in range(len(chunks) + len(stages) - 1):
        for depth, stage in enumerate(stages):
            c = wave - depth
            if 0 <= c < len(chunks):
                stage(c)
    if emit_state:
        c_out_ref[...] = ct_ref[...]
        n_out_ref[...] = n_ref[...]
        m_out_ref[...] = m_ref[...]


def _mixer(yst, q, k, vt, ot, gst, gmt, gcol, grow, wglut, mngt, wmot, wo, c0, n0, m0,
           *, n_pad, emit_state):
    bsz, d_model, length = gst.shape
    tile = min(TILE_MIX, length)
    chunk = MLSTM_CHUNK
    assert length % tile == 0 and tile % chunk == 0 and mngt.shape[1] == chunk
    d_qk = q.shape[-1]
    d_v = vt.shape[1]
    d_s5 = yst.shape[1]
    dk, dv = d_qk // M_HEADS, d_v // M_HEADS
    grid = (bsz, length // tile)

    def tok(width):
        return pl.BlockSpec((None, tile, width), lambda b, j: (b, j, 0))

    def feat(rows):
        return pl.BlockSpec((None, rows, tile), lambda b, j: (b, 0, j))

    consts = [wglut, mngt, wmot, wo, c0, n0, m0]
    in_specs = ([feat(d_s5), tok(d_qk), tok(d_qk), feat(d_v), feat(d_v), feat(d_model),
                 feat(d_model), tok(LANES), feat(2 * M_HEADS)]
                + [_const_spec(c.shape) for c in consts])
    out_specs = [tok(d_model)]
    out_shape = [jax.ShapeDtypeStruct((bsz, length, d_model), F32)]
    if emit_state:
        out_specs += [pl.BlockSpec((None, M_HEADS, dv, dk), lambda b, j: (b, 0, 0, 0)),
                      pl.BlockSpec((None, M_HEADS, 8, dk), lambda b, j: (b, 0, 0, 0)),
                      pl.BlockSpec((None, 8, dk), lambda b, j: (b, 0, 0))]
        out_shape += [jax.ShapeDtypeStruct((bsz, M_HEADS, dv, dk), F32),
                      jax.ShapeDtypeStruct((bsz, M_HEADS, 8, dk), F32),
                      jax.ShapeDtypeStruct((bsz, 8, dk), F32)]
    kern = functools.partial(_mixer_kernel, tile=tile, chunk=chunk, n_pad=n_pad,
                             emit_state=emit_state, d_model=d_model)
    return pl.pallas_call(
        kern,
        grid=grid,
        in_specs=in_specs,
        out_specs=out_specs,
        out_shape=out_shape,
        scratch_shapes=[pltpu.VMEM((M_HEADS, dv, dk), F32), pltpu.VMEM((M_HEADS, 8, dk), F32),
                        pltpu.VMEM((8, dk), F32), pltpu.VMEM((d_v, tile), BF16)],
        compiler_params=pltpu.CompilerParams(
            dimension_semantics=("arbitrary", "arbitrary"), vmem_limit_bytes=V7X_VMEM_LIMIT),
        name="mixer",
    )(yst, q, k, vt, ot, gst, gmt, gcol, grow, *consts)


def _ffn_kernel(x_ref, pre_ref, g0_ref, b0_ref, g1_ref, b1_ref, wu_ref, bu_ref, wd_ref, g2_ref,
                b2_ref, out_ref, *, alpha, ff_chunk, row_block):
    d_ff = wu_ref.shape[1]
    tile = x_ref.shape[0]
    for r0 in range(0, tile, row_block):
        rows = slice(r0, r0 + row_block)
        h0 = _layer_norm(x_ref[rows, :], g0_ref[...], b0_ref[...])
        h = _layer_norm(alpha * h0 + pre_ref[rows, :], g1_ref[...], b1_ref[...])
        hb = h.astype(BF16)
        acc = alpha * h
        for c in range(d_ff // ff_chunk):
            lo = c * ff_chunk
            a = jnp.maximum(
                _dot(hb, wu_ref[:, lo:lo + ff_chunk]) + bu_ref[:, lo:lo + ff_chunk], 0.0)
            acc = acc + _dot((a * a).astype(BF16), wd_ref[lo:lo + ff_chunk, :])
        out_ref[rows, :] = _layer_norm(acc, g2_ref[...], b2_ref[...])


def _ffn(x, pre, g0, b0, g1, b1, wu, bu, wd, g2, b2, *, alpha):
    rows, d_model = x.shape
    tile = TILE_FFN
    assert rows % tile == 0
    d_ff = wu.shape[1]
    return pl.pallas_call(
        functools.partial(_ffn_kernel, alpha=alpha, ff_chunk=min(2048, d_ff),
                          row_block=min(ROW_BLOCK, tile)),
        grid=(rows // tile,),
        in_specs=[pl.BlockSpec((tile, d_model), lambda i: (i, 0)),
                  pl.BlockSpec((tile, d_model), lambda i: (i, 0)),
                  _const_spec((1, d_model)), _const_spec((1, d_model)),
                  _const_spec((1, d_model)), _const_spec((1, d_model)),
                  _const_spec((d_model, d_ff)), _const_spec((1, d_ff)),
                  _const_spec((d_ff, d_model)), _const_spec((1, d_model)),
                  _const_spec((1, d_model))],
        out_specs=pl.BlockSpec((tile, d_model), lambda i: (i, 0)),
        out_shape=jax.ShapeDtypeStruct((rows, d_model), F32),
        compiler_params=pltpu.CompilerParams(
            dimension_semantics=("arbitrary",), vmem_limit_bytes=V7X_VMEM_LIMIT),
        name="ffn",
    )(x, pre, g0, b0, g1, b1, wu, bu, wd, g2, b2)


def kernel(x, meta_tokens, ln0_g, ln0_b, w_in, b_in, qk_conv_w, qk_conv_b, s5_lambda_re, s5_lambda_im, s5_log_dt, s5_b_re, s5_b_im, s5_c_re, s5_c_im, s5_d, s5_w_glu, m_norm_g, m_w_out, w_o, ln1_g, ln1_b, w_up, b_up, w_down, ln2_g, ln2_b):
    bsz, seq, d_model = x.shape
    depth = w_in.shape[0]
    assert depth == 1, "the meta-token prologue is written for a single layer"
    assert meta_tokens.shape == (N_META, d_model)
    assert seq % TILE_MIX == 0 and seq % S5_CHUNK == 0 and MLSTM_CHUNK >= S5_CHUNK >= N_META
    alpha = (2.0 * depth) ** 0.25
    d_s5 = d_model // 2
    d_qk = qk_conv_w.shape[-1] // 2
    d_v = m_norm_g.shape[-1]
    dk, dv = d_qk // M_HEADS, d_v // M_HEADS
    gate_lo = d_s5 + 2 * d_qk + 2 * d_v
    gate_hi = gate_lo + 2 * M_HEADS
    assert w_in.shape[-1] == gate_hi + 2 * d_model

    row = lambda a: a.reshape(1, -1).astype(F32)
    w = w_in[0]
    bias = b_in[0]
    qk_lo, qk_hi = d_s5, d_s5 + 2 * d_qk
    w_t = jnp.concatenate([w[:, :qk_lo], w[:, qk_hi:gate_lo], w[:, gate_hi:]], axis=1).T.astype(BF16)
    b_t = jnp.concatenate([bias[:qk_lo], bias[qk_hi:gate_lo], bias[gate_hi:]]).reshape(-1, 1).astype(F32)
    w_qk = w[:, qk_lo:qk_hi].astype(BF16)
    b_qk = row(bias[qk_lo:qk_hi])
    w_gate = w[:, gate_lo:gate_hi]
    wgc = jnp.pad(w_gate, ((0, 0), (0, LANES - 2 * M_HEADS))).astype(BF16)
    bgc = jnp.pad(bias[gate_lo:gate_hi], (0, LANES - 2 * M_HEADS)).reshape(1, LANES).astype(F32)
    wgr = w_gate.T.astype(BF16)
    bgr = bias[gate_lo:gate_hi].reshape(2 * M_HEADS, 1).astype(F32)
    g0, b0 = row(ln0_g), row(ln0_b)
    conv_w = qk_conv_w[0].astype(F32)
    conv_b = row(qk_conv_b[0])
    mngt = jnp.broadcast_to(m_norm_g[0].astype(F32)[:, None], (d_v, MLSTM_CHUNK))
    mixer_consts = (s5_w_glu[0].T.astype(BF16), mngt, m_w_out[0].T.astype(BF16),
                    w_o[0].astype(BF16))
    inproj_consts = (g0, b0, w_t, b_t, w_qk, b_qk, conv_w, conv_b, wgc, bgc, wgr, bgr)

    pad = MLSTM_CHUNK - N_META
    x_meta = jnp.concatenate([jnp.zeros((pad, d_model), x.dtype), meta_tokens.astype(x.dtype)])[None]
    ut_m, q_m, k_m, v_m, o_m, gs_m, gm_m, gcol_m, grow_m, carry_m = _inproj(
        x_meta, *inproj_consts, jnp.zeros((CARRY_ROWS, 2 * d_qk), F32), n_pad=pad)
    zero_state = (jnp.zeros((M_HEADS, dv, dk), F32), jnp.zeros((M_HEADS, 8, dk), F32),
                  jnp.zeros((8, dk), F32))
    _, c_m, n_m, m_m = _mixer(
        jnp.zeros((1, d_s5, MLSTM_CHUNK), BF16), q_m, k_m, v_m, o_m, gs_m, gm_m, gcol_m, grow_m,
        *mixer_consts, *zero_state, n_pad=pad, emit_state=True)

    ut, q, k, v, o, gs, gm, gcol, grow, _ = _inproj(x, *inproj_consts, carry_m[0], n_pad=0)
    s5_consts = _s5_weights(s5_lambda_re[0], s5_lambda_im[0], s5_log_dt[0], s5_b_re[0], s5_b_im[0],
                            s5_c_re[0], s5_c_im[0], s5_d[0])
    yst = _s5(ut, ut_m[0, :, MLSTM_CHUNK - S5_CHUNK:], *s5_consts)
    (pre,) = _mixer(yst, q, k, v, o, gs, gm, gcol, grow, *mixer_consts, c_m[0], n_m[0], m_m[0],
                    n_pad=0, emit_state=False)
    out = _ffn(x.reshape(bsz * seq, d_model), pre.reshape(bsz * seq, d_model), g0, b0,
               row(ln1_g[0]), row(ln1_b[0]), w_up[0].astype(BF16), row(b_up[0]),
               w_down[0].astype(BF16), row(ln2_g[0]), row(ln2_b[0]), alpha=alpha)
    return out.reshape(bsz, seq, d_model)
```

```python
import functools
import math

import jax
import jax.numpy as jnp
from jax import lax
from jax.experimental import pallas as pl
from jax.experimental.pallas import tpu as pltpu

F32 = jnp.float32
BF16 = jnp.bfloat16

N_META = 16
S5_GROUP = 16
S5_STATE = 64
M_HEADS = 4
CONV_WIDTH = 4
LN_EPS = 1e-5

LANES = 128
V7X_MXU_WIDTH = 256
S5_CHUNK = LANES
TILE_IN = 512
MLSTM_CHUNK = 256
TILE_MIX = 1024
TILE_FFN = 512
ROW_BLOCK = 256
CARRY_ROWS = 8
V7X_VMEM_LIMIT = 56 * 1024 * 1024

NEG_INF = float("-inf")


def _layer_norm(x, g, b):
    mu = jnp.mean(x, axis=-1, keepdims=True)
    xc = x - mu
    var = jnp.mean(xc * xc, axis=-1, keepdims=True)
    return xc * lax.rsqrt(var + LN_EPS) * g + b


def _sigmoid(x):
    return 0.5 * jnp.tanh(0.5 * x) + 0.5


def _log_sigmoid(x):
    return jnp.minimum(x, 0.0) - jnp.log1p(jnp.exp(-jnp.abs(x)))


def _gelu_tanh(x):
    c = math.sqrt(2.0 / math.pi)
    return x * (0.5 * (1.0 + jnp.tanh(c * (x + 0.044715 * (x * x * x)))))


def _dot(a, b):
    return jnp.dot(a, b, preferred_element_type=F32)


def _dot_nt(a, b):
    return lax.dot_general(a, b, (((1,), (1,)), ((), ())), preferred_element_type=F32)


def _dot_tn(a, b):
    return lax.dot_general(a, b, (((0,), (0,)), ((), ())), preferred_element_type=F32)


def _split3(x):
    hi = x.astype(BF16)
    r = x - hi.astype(F32)
    mid = r.astype(BF16)
    lo = (r - mid.astype(F32)).astype(BF16)
    return hi, mid, lo


def _const_spec(shape):
    nd = len(shape)
    return pl.BlockSpec(shape, lambda *_: (0,) * nd, pipeline_mode=pl.Buffered(1))


def _inproj_kernel(x_ref, g0_ref, b0_ref, wt_ref, bt_ref, w_ref, b_ref, cw_ref, cb_ref,
                   wgc_ref, bgc_ref, wgr_ref, bgr_ref, carry_in_ref,
                   ut_ref, q_ref, k_ref, vt_ref, ot_ref, gst_ref, gmt_ref, gcol_ref, grow_ref,
                   carry_out_ref,
                   ext_ref, *, tile, row_block, n_pad, d_qk, d_s5, d_v, d_model):
    j = pl.program_id(1)

    @pl.when(j == 0)
    def _():
        ext_ref[0:CARRY_ROWS, :] = carry_in_ref[...]

    blocks = [slice(r0, r0 + row_block) for r0 in range(0, tile, row_block)]

    def seg_t(lo, n_rows, rows, hb):
        return _dot_nt(wt_ref[lo:lo + n_rows, :], hb) + bt_ref[lo:lo + n_rows, :]

    hbs = []
    for rows in blocks:
        hb = _layer_norm(x_ref[rows, :], g0_ref[...], b0_ref[...]).astype(BF16)
        hbs.append(hb)
        qk = _dot(hb, w_ref[...]) + b_ref[...]
        if n_pad:
            t_idx = lax.broadcasted_iota(jnp.int32, (row_block, 1), 0) + (j * tile + rows.start)
            qk = jnp.where(t_idx >= n_pad, qk, 0.0)
        ext_ref[CARRY_ROWS + rows.start:CARRY_ROWS + rows.stop, :] = qk
        gcol_ref[rows, :] = _dot(hb, wgc_ref[...]) + bgc_ref[...]
        grow_ref[:, rows] = _dot_nt(wgr_ref[...], hb) + bgr_ref[...]
        ut = seg_t(0, d_s5, rows, hb)
        if n_pad:
            cols = lax.broadcasted_iota(jnp.int32, (1, row_block), 1) + (j * tile + rows.start)
            ut = jnp.where(cols >= n_pad, ut, 0.0)
        ut_ref[:, rows] = ut.astype(BF16)
        vt_ref[:, rows] = seg_t(d_s5, d_v, rows, hb).astype(BF16)
    off = d_s5 + d_v

    ext = ext_ref[...]
    conv = cb_ref[...] + cw_ref[CONV_WIDTH - 1:CONV_WIDTH, :] * ext[CARRY_ROWS:, :]
    for r in range(CONV_WIDTH - 1):
        lag = CONV_WIDTH - 1 - r
        conv = conv + cw_ref[r:r + 1, :] * pltpu.roll(ext, lag, 0)[CARRY_ROWS:, :]
    tail = ext[tile:, :]
    ext_ref[0:CARRY_ROWS, :] = tail
    carry_out_ref[...] = tail
    act = conv * _sigmoid(conv)
    q_ref[...] = act[:, :d_qk].astype(BF16)
    k_ref[...] = (act[:, d_qk:] * ((d_qk // M_HEADS) ** -0.5)).astype(BF16)

    for out_ref, n_rows in ((ot_ref, d_v), (gst_ref, d_model), (gmt_ref, d_model)):
        for rows, hb in zip(blocks, hbs):
            out_ref[:, rows] = _sigmoid(seg_t(off, n_rows, rows, hb)).astype(BF16)
        off += n_rows


def _inproj(x, g0, b0, w_t, b_t, w_qk, b_qk, conv_w, conv_b, wgc, bgc, wgr, bgr, carry_in,
            *, n_pad):
    bsz, length, d_model = x.shape
    tile = min(TILE_IN, length)
    assert length % tile == 0
    d_qk = conv_w.shape[1] // 2
    d_s5 = d_model // 2
    d_v = d_model
    n_t = w_t.shape[0]
    assert n_t == d_s5 + 2 * d_v + 2 * d_model and w_qk.shape[1] == 2 * d_qk
    grid = (bsz, length // tile)

    def tok(width):
        return pl.BlockSpec((None, tile, width), lambda b, j: (b, j, 0))

    def feat(rows):
        return pl.BlockSpec((None, rows, tile), lambda b, j: (b, 0, j))

    def feat_out(rows):
        return jax.ShapeDtypeStruct((bsz, rows, length), BF16)

    kern = functools.partial(_inproj_kernel, tile=tile, row_block=min(ROW_BLOCK, tile), n_pad=n_pad,
                             d_qk=d_qk, d_s5=d_s5, d_v=d_v, d_model=d_model)
    return pl.pallas_call(
        kern,
        grid=grid,
        in_specs=[tok(d_model), _const_spec((1, d_model)), _const_spec((1, d_model)),
                  _const_spec((n_t, d_model)), _const_spec((n_t, 1)),
                  _const_spec((d_model, 2 * d_qk)), _const_spec((1, 2 * d_qk)),
                  _const_spec((CONV_WIDTH, 2 * d_qk)), _const_spec((1, 2 * d_qk)),
                  _const_spec(wgc.shape), _const_spec(bgc.shape), _const_spec(wgr.shape),
                  _const_spec(bgr.shape), _const_spec((CARRY_ROWS, 2 * d_qk))],
        out_specs=[feat(d_s5), tok(d_qk), tok(d_qk), feat(d_v), feat(d_v), feat(d_model),
                   feat(d_model), tok(LANES), feat(2 * M_HEADS),
                   pl.BlockSpec((None, CARRY_ROWS, 2 * d_qk), lambda b, j: (b, 0, 0))],
        out_shape=[feat_out(d_s5),
                   jax.ShapeDtypeStruct((bsz, length, d_qk), BF16),
                   jax.ShapeDtypeStruct((bsz, length, d_qk), BF16),
                   feat_out(d_v), feat_out(d_v), feat_out(d_model), feat_out(d_model),
                   jax.ShapeDtypeStruct((bsz, length, LANES), F32),
                   jax.ShapeDtypeStruct((bsz, 2 * M_HEADS, length), F32),
                   jax.ShapeDtypeStruct((bsz, CARRY_ROWS, 2 * d_qk), F32)],
        scratch_shapes=[pltpu.VMEM((CARRY_ROWS + tile, 2 * d_qk), F32)],
        compiler_params=pltpu.CompilerParams(
            dimension_semantics=("arbitrary", "arbitrary"), vmem_limit_bytes=V7X_VMEM_LIMIT),
        name="inproj",
    )(x, g0, b0, w_t, b_t, w_qk, b_qk, conv_w, conv_b, wgc, bgc, wgr, bgr, carry_in)


def _s5_kernel(ut_ref, um_ref, taps_ref, wst_ref, wc_ref, a1_ref, a2_ref, dsk_ref, yt_ref,
               px_ref, lhs_ref, toe_ref, s_ref, xin_ref, y_ref, ymid_ref, *, n_chunks, bsz):
    t_len = S5_CHUNK
    n_ch = S5_GROUP
    length = n_chunks * t_len
    half = 2 * S5_STATE
    rows_bj = bsz * n_ch
    lane_step = 1024

    r_idx = lax.broadcasted_iota(jnp.int32, (rows_bj, rows_bj), 0)
    q_idx = lax.broadcasted_iota(jnp.int32, (rows_bj, rows_bj), 1)
    perm = jnp.where(q_idx == (r_idx % bsz) * n_ch + r_idx // bsz, 1.0, 0.0).astype(BF16)
    perm_back = jnp.where(q_idx == (r_idx % n_ch) * bsz + r_idx // n_ch, 1.0, 0.0).astype(BF16)
    x_bj = ut_ref[...].reshape(rows_bj, length)
    for lo in range(0, length, lane_step):
        px_ref[:, lo:lo + lane_step] = _dot(perm, x_bj[:, lo:lo + lane_step]).astype(BF16)

    for jj in range(n_ch):
        lhs_ref[0:bsz, jj * t_len:(jj + 1) * t_len] = jnp.broadcast_to(
            um_ref[jj:jj + 1, :], (bsz, t_len))
    for c in range(n_chunks):
        for jj in range(n_ch):
            lhs_ref[(c + 1) * bsz:(c + 2) * bsz, jj * t_len:(jj + 1) * t_len] = (
                px_ref[jj * bsz:(jj + 1) * bsz, c * t_len:(c + 1) * t_len])

    lhs = lhs_ref[...]
    wst = wst_ref[...].reshape(n_ch * t_len, 2 * half)
    s_ref[...] = _dot(lhs, wst)
    a1 = a1_ref[...]
    a2 = a2_ref[...]
    state = jnp.zeros((bsz, 2 * half), F32)
    for c in range(n_chunks + 1):
        xin_ref[c * bsz:(c + 1) * bsz, :] = state[:, :half]
        swapped = jnp.concatenate([state[:, half:], state[:, :half]], axis=1)
        state = a1 * state + a2 * swapped + s_ref[c * bsz:(c + 1) * bsz, :]
    xin = xin_ref[...].astype(BF16)

    s_io = lax.broadcasted_iota(jnp.int32, (t_len, t_len), 0)
    t_io = lax.broadcasted_iota(jnp.int32, (t_len, t_len), 1)
    lower = t_io >= s_io
    cols_per_step = V7X_MXU_WIDTH // t_len
    for i0 in range(0, n_ch, cols_per_step):
        cols = slice(i0 * t_len, (i0 + cols_per_step) * t_len)
        for ii in range(i0, i0 + cols_per_step):
            for jj in range(n_ch):
                tap = jnp.broadcast_to(taps_ref[jj, ii:ii + 1, :], (t_len, t_len))
                blk = pltpu.roll(tap, 0, 1, stride=1, stride_axis=0)
                toe_ref[jj * t_len:(jj + 1) * t_len, ii * t_len:(ii + 1) * t_len] = (
                    jnp.where(lower, blk, 0.0).astype(BF16))
        wc_cols = jnp.concatenate([wc_ref[ii] for ii in range(i0, i0 + cols_per_step)], axis=1)
        y_ref[:, cols] = (_dot(lhs, toe_ref[:, cols]) + _dot(xin, wc_cols)
                          + lhs[:, cols].astype(F32) * dsk_ref[:, cols])

    for c in range(n_chunks):
        for ii in range(n_ch):
            ymid_ref[ii * bsz:(ii + 1) * bsz, c * t_len:(c + 1) * t_len] = (
                y_ref[(c + 1) * bsz:(c + 2) * bsz, ii * t_len:(ii + 1) * t_len].astype(BF16))
    for lo in range(0, length, lane_step):
        yt_ref[:, :, lo:lo + lane_step] = _dot(
            perm_back, ymid_ref[:, lo:lo + lane_step]).astype(BF16).reshape(bsz, n_ch, lane_step)


def _s5(ut, um, taps, wst, wc, a1, a2, dskip):
    bsz, d_s5, length = ut.shape
    groups = d_s5 // S5_GROUP
    t_len = S5_CHUNK
    n_chunks = length // t_len
    assert length % 1024 == 0
    width = S5_GROUP * t_len
    half = 2 * S5_STATE
    rows = (n_chunks + 1) * bsz

    def grp(*shape):
        return pl.BlockSpec((None,) + shape, lambda g: (g,) + (0,) * len(shape))

    return pl.pallas_call(
        functools.partial(_s5_kernel, n_chunks=n_chunks, bsz=bsz),
        grid=(groups,),
        in_specs=[pl.BlockSpec((bsz, S5_GROUP, length), lambda g: (0, g, 0)),
                  pl.BlockSpec((S5_GROUP, t_len), lambda g: (g, 0)),
                  grp(S5_GROUP, S5_GROUP, t_len), grp(S5_GROUP, t_len, 2 * half), grp(S5_GROUP, half, t_len),
                  grp(1, 2 * half), grp(1, 2 * half), grp(1, width)],
        out_specs=pl.BlockSpec((bsz, S5_GROUP, length), lambda g: (0, g, 0)),
        out_shape=jax.ShapeDtypeStruct((bsz, d_s5, length), BF16),
        scratch_shapes=[pltpu.VMEM((bsz * S5_GROUP, length), BF16),
                        pltpu.VMEM((rows, width), BF16),
                        pltpu.VMEM((width, width), BF16),
                        pltpu.VMEM((rows, 2 * half), F32),
                        pltpu.VMEM((rows, half), F32),
                        pltpu.VMEM((rows, width), F32),
                        pltpu.VMEM((bsz * S5_GROUP, length), BF16)],
        compiler_params=pltpu.CompilerParams(
            dimension_semantics=("arbitrary",), vmem_limit_bytes=V7X_VMEM_LIMIT),
        name="s5",
    )(ut, um, taps, wst, wc, a1, a2, dskip)


def _s5_weights(lam_re, lam_im, log_dt, b_re, b_im, c_re, c_im, d_skip):
    hp = lax.Precision.HIGHEST
    t_len = S5_CHUNK
    groups, n_state = lam_re.shape
    lr, li = lam_re.astype(F32), lam_im.astype(F32)
    dt = jnp.exp(log_dt.astype(F32))[:, None]
    ar, ai = lr * dt, li * dt
    steps = jnp.arange(t_len + 1, dtype=F32)[:, None, None]
    mag = jnp.exp(ar[None] * steps)
    pr, pi = mag * jnp.cos(ai[None] * steps), mag * jnp.sin(ai[None] * steps)
    nr, ni = pr[1] - 1.0, pi[1]
    den = lr * lr + li * li
    fr, fi = (nr * lr + ni * li) / den, (ni * lr - nr * li) / den
    bre, bim = b_re.astype(F32), b_im.astype(F32)
    bbr = fr[..., None] * bre - fi[..., None] * bim
    bbi = fr[..., None] * bim + fi[..., None] * bre
    cre, cim = c_re.astype(F32), c_im.astype(F32)
    cbr = cre[:, :, None, :] * jnp.transpose(bbr, (0, 2, 1))[:, None] \
        - cim[:, :, None, :] * jnp.transpose(bbi, (0, 2, 1))[:, None]
    cbi = cre[:, :, None, :] * jnp.transpose(bbi, (0, 2, 1))[:, None] \
        + cim[:, :, None, :] * jnp.transpose(bbr, (0, 2, 1))[:, None]
    taps = (jnp.einsum("gijp,dgp->gjid", cbr, pr[:t_len], precision=hp)
            - jnp.einsum("gijp,dgp->gjid", cbi, pi[:t_len], precision=hp))

    prg = jnp.transpose(pr[:t_len][::-1], (1, 0, 2))
    pig = jnp.transpose(pi[:t_len][::-1], (1, 0, 2))
    bjr = jnp.transpose(bbr, (0, 2, 1))
    bji = jnp.transpose(bbi, (0, 2, 1))
    p_re4 = jnp.tile(prg, (1, 1, 4))[:, None]
    p_im4 = jnp.tile(pig, (1, 1, 4))[:, None]
    b_u = jnp.concatenate([bjr, bji, bji, bjr], axis=-1)[:, :, None]
    b_v = jnp.concatenate([-bji, bjr, bjr, -bji], axis=-1)[:, :, None]
    wst = (p_re4 * b_u + p_im4 * b_v).astype(BF16)

    ptr = jnp.transpose(pr[1:], (1, 2, 0))
    pti = jnp.transpose(pi[1:], (1, 2, 0))
    ctr = jnp.transpose(cre, (0, 2, 1))
    cti = jnp.transpose(cim, (0, 2, 1))
    p_re2 = jnp.tile(ptr, (1, 2, 1))[:, None]
    p_im2 = jnp.tile(pti, (1, 2, 1))[:, None]
    c_u = jnp.transpose(jnp.concatenate([ctr, -cti], axis=1), (0, 2, 1))[..., None]
    c_v = jnp.transpose(jnp.concatenate([-cti, -ctr], axis=1), (0, 2, 1))[..., None]
    wc = (c_u * p_re2 + c_v * p_im2).astype(BF16)

    a_re, a_im = pr[t_len], pi[t_len]
    a1 = jnp.concatenate([a_re, a_re, a_re, a_re], axis=-1)[:, None, :]
    a2 = jnp.concatenate([-a_im, a_im, a_im, -a_im], axis=-1)[:, None, :]
    dsk = jnp.repeat(d_skip.astype(F32).reshape(groups, 1, S5_GROUP), t_len, axis=-1)
    return taps, wst, wc, a1, a2, dsk


def _mixer_kernel(yst_ref, q_ref, k_ref, vt_ref, ot_ref, gst_ref, gmt_ref, gcol_ref, grow_ref,
                  wglut_ref, mngt_ref, wmot_ref, wo_ref, c0_ref, n0_ref, m0_ref,
                  *refs, tile, chunk, n_pad, emit_state, d_model):
    if emit_state:
        out_ref, c_out_ref, n_out_ref, m_out_ref, ct_ref, n_ref, m_ref, hmt_ref = refs
    else:
        out_ref, ct_ref, n_ref, m_ref, hmt_ref = refs
    j = pl.program_id(1)
    dk = q_ref.shape[-1] // M_HEADS
    dv = vt_ref.shape[0] // M_HEADS

    @pl.when(j == 0)
    def _():
        ct_ref[...] = c0_ref[...]
        n_ref[...] = n0_ref[...]
        m_ref[...] = m0_ref[...]

    heads = range(M_HEADS)
    chunks = [slice(c0, c0 + chunk) for c0 in range(0, tile, chunk)]
    ti = lax.broadcasted_iota(jnp.int32, (chunk, chunk), 0)
    tj = lax.broadcasted_iota(jnp.int32, (chunk, chunk), 1)
    tri_lower = jnp.where(ti >= tj, 1.0, 0.0).astype(BF16)
    tri_upper = jnp.where(ti <= tj, 1.0, 0.0).astype(BF16)

    qs = [[q_ref[cs, h * dk:(h + 1) * dk] for h in heads] for cs in chunks]
    ks = [[k_ref[cs, h * dk:(h + 1) * dk] for h in heads] for cs in chunks]
    vts = [[vt_ref[h * dv:(h + 1) * dv, cs] for h in heads] for cs in chunks]
    state = {"m": [m_ref[h:h + 1, 0:1] for h in heads],
             "ct": [ct_ref[h] for h in heads],
             "n": [n_ref[h] for h in heads]}
    kq, zt, m_sts, cq, nq, brs, r_cols = {}, {}, {}, {}, {}, {}, {}

    def stage1(c):
        cs = chunks[c]
        m_st, ct_st, n_st = state["m"], state["ct"], state["n"]
        kq[c] = [_dot_nt(ks[c][h], qs[c][h]) for h in heads]
        zt[c] = _dot(wglut_ref[...], _gelu_tanh(yst_ref[:, cs].astype(F32)).astype(BF16))
        g_col = gcol_ref[cs, :]
        g_row = grow_ref[:, cs]
        lane = lax.broadcasted_iota(jnp.int32, g_col.shape, 1)
        sub = lax.broadcasted_iota(jnp.int32, g_row.shape, 0)
        lf_col = jnp.where((lane >= M_HEADS) & (lane < 2 * M_HEADS), _log_sigmoid(g_col), 0.0)
        lf_row = jnp.where(sub >= M_HEADS, _log_sigmoid(g_row), 0.0)
        li_col, li_row = g_col, g_row
        if n_pad:
            t_col = lax.broadcasted_iota(jnp.int32, g_col.shape, 0) + (j * tile + cs.start)
            t_row = lax.broadcasted_iota(jnp.int32, g_row.shape, 1) + (j * tile + cs.start)
            lf_col = jnp.where(t_col >= n_pad, lf_col, 0.0)
            lf_row = jnp.where(t_row >= n_pad, lf_row, 0.0)
            li_col = jnp.where(t_col >= n_pad, li_col, NEG_INF)
            li_row = jnp.where(t_row >= n_pad, li_row, NEG_INF)
        b_col = sum(_dot(tri_lower, part) for part in _split3(lf_col))
        b_row = sum(_dot(part, tri_upper) for part in _split3(lf_row))

        brs[c] = [b_row[M_HEADS + h:M_HEADS + h + 1, :] for h in heads]
        r_cols[c] = [li_col[:, h:h + 1] - b_col[:, M_HEADS + h:M_HEADS + h + 1] for h in heads]
        r_rows = [li_row[h:h + 1, :] - brs[c][h] for h in heads]

        m_sts[c] = m_st
        cq[c] = [_dot_nt(ct_st[h].astype(BF16), qs[c][h]) for h in heads]
        nq[c] = [_dot_nt(n_st[h].astype(BF16), qs[c][h])[0:1, :] for h in heads]
        a_last = [jnp.maximum(m_st[h], jnp.max(r_rows[h], axis=-1, keepdims=True)) for h in heads]
        w_k = [jnp.exp(r_rows[h] - a_last[h]) for h in heads]
        upd = [_dot((vts[c][h].astype(F32) * w_k[h]).astype(BF16), ks[c][h]) for h in heads]
        n_upd = [_dot(jnp.broadcast_to(w_k[h], (8, chunk)).astype(BF16), ks[c][h]) for h in heads]
        decay = [jnp.exp(m_st[h] - a_last[h]) for h in heads]
        state["ct"] = [decay[h] * ct_st[h] + upd[h] for h in heads]
        state["n"] = [decay[h] * n_st[h] + n_upd[h] for h in heads]
        state["m"] = [brs[c][h][:, chunk - 1:chunk] + a_last[h] for h in heads]
        if c == len(chunks) - 1:
            for h in heads:
                ct_ref[h] = state["ct"][h]
                n_ref[h] = state["n"][h]
                m_ref[h:h + 1, :] = jnp.broadcast_to(state["m"][h], (1, m_ref.shape[1]))

    a_rows, w_inters, s_mats, vs_t, y_m = {}, {}, {}, {}, {}

    def stage2(c):
        e_mats = [jnp.where(ti <= tj, r_cols[c][h], NEG_INF) for h in heads]
        a_rows[c] = [jnp.maximum(m_sts[c][h], jnp.max(e_mats[h], axis=0, keepdims=True))
                     for h in heads]
        w_inters[c] = [jnp.exp(m_sts[c][h] - a_rows[c][h]) for h in heads]
        s_mats[c] = [kq[c][h] * jnp.exp(e_mats[h] - a_rows[c][h]) for h in heads]
        vs_t[c] = [_dot(vts[c][h], s_mats[c][h].astype(BF16)) for h in heads]

    def stage3(c):
        cs = chunks[c]
        for h in heads:
            den = jnp.sum(s_mats[c][h], axis=0, keepdims=True) + w_inters[c][h] * nq[c][h]
            scale = 1.0 / jnp.maximum(jnp.abs(den), jnp.exp(-(brs[c][h] + a_rows[c][h])))
            hh = (vs_t[c][h] + w_inters[c][h] * cq[c][h]) * scale
            hc = hh - jnp.mean(hh, axis=0, keepdims=True)
            var = jnp.mean(hc * hc, axis=0, keepdims=True)
            hn = hc * lax.rsqrt(var + LN_EPS) * mngt_ref[h * dv:(h + 1) * dv, :]
            hmt_ref[h * dv:(h + 1) * dv, cs] = (
                ot_ref[h * dv:(h + 1) * dv, cs].astype(F32) * hn).astype(BF16)
        y_m[c] = _dot(wmot_ref[...], hmt_ref[:, cs])

    def stage4(c):
        cs = chunks[c]
        y_s5 = zt[c][:d_model, :] * _sigmoid(zt[c][d_model:, :])
        mix = gst_ref[:, cs].astype(F32) * y_s5 + gmt_ref[:, cs].astype(F32) * y_m[c]
        out_ref[cs, :] = _dot_tn(mix.astype(BF16), wo_ref[...])

    stages = (stage1, stage2, stage3, stage4)
    for wave in range(len(chunks) + len(stages) - 1):
        for depth, stage in enumerate(stages):
            c = wave - depth
            if 0 <= c < len(chunks):
                stage(c)
    if emit_state:
        c_out_ref[...] = ct_ref[...]
        n_out_ref[...] = n_ref[...]
        m_out_ref[...] = m_ref[...]


def _mixer(yst, q, k, vt, ot, gst, gmt, gcol, grow, wglut, mngt, wmot, wo, c0, n0, m0,
           *, n_pad, emit_state):
    bsz, d_model, length = gst.shape
    tile = min(TILE_MIX, length)
    chunk = MLSTM_CHUNK
    assert length % tile == 0 and tile % chunk == 0 and mngt.shape[1] == chunk
    d_qk = q.shape[-1]
    d_v = vt.shape[1]
    d_s5 = yst.shape[1]
    dk, dv = d_qk // M_HEADS, d_v // M_HEADS
    grid = (bsz, length // tile)

    def tok(width):
        return pl.BlockSpec((None, tile, width), lambda b, j: (b, j, 0))

    def feat(rows):
        return pl.BlockSpec((None, rows, tile), lambda b, j: (b, 0, j))

    consts = [wglut, mngt, wmot, wo, c0, n0, m0]
    in_specs = ([feat(d_s5), tok(d_qk), tok(d_qk), feat(d_v), feat(d_v), feat(d_model),
                 feat(d_model), tok(LANES), feat(2 * M_HEADS)]
                + [_const_spec(c.shape) for c in consts])
    out_specs = [tok(d_model)]
    out_shape = [jax.ShapeDtypeStruct((bsz, length, d_model), F32)]
    if emit_state:
        out_specs += [pl.BlockSpec((None, M_HEADS, dv, dk), lambda b, j: (b, 0, 0, 0)),
                      pl.BlockSpec((None, M_HEADS, 8, dk), lambda b, j: (b, 0, 0, 0)),
                      pl.BlockSpec((None, 8, dk), lambda b, j: (b, 0, 0))]
        out_shape += [jax.ShapeDtypeStruct((bsz, M_HEADS, dv, dk), F32),
                      jax.ShapeDtypeStruct((bsz, M_HEADS, 8, dk), F32),
                      jax.ShapeDtypeStruct((bsz, 8, dk), F32)]
    kern = functools.partial(_mixer_kernel, tile=tile, chunk=chunk, n_pad=n_pad,
                             emit_state=emit_state, d_model=d_model)
    return pl.pallas_call(
        kern,
        grid=grid,
        in_specs=in_specs,
        out_specs=out_specs,
        out_shape=out_shape,
        scratch_shapes=[pltpu.VMEM((M_HEADS, dv, dk), F32), pltpu.VMEM((M_HEADS, 8, dk), F32),
                        pltpu.VMEM((8, dk), F32), pltpu.VMEM((d_v, tile), BF16)],
        compiler_params=pltpu.CompilerParams(
            dimension_semantics=("arbitrary", "arbitrary"), vmem_limit_bytes=V7X_VMEM_LIMIT),
        name="mixer",
    )(yst, q, k, vt, ot, gst, gmt, gcol, grow, *consts)


def _ffn_kernel(x_ref, pre_ref, g0_ref, b0_ref, g1_ref, b1_ref, wu_ref, bu_ref, wd_ref, g2_ref,
                b2_ref, out_ref, *, alpha, ff_chunk, row_block):
    d_ff = wu_ref.shape[1]
    tile = x_ref.shape[0]
    blocks = [slice(r0, r0 + row_block) for r0 in range(0, tile, row_block)]
    hbs, accs = [], []
    for rows in blocks:
        h0 = _layer_norm(x_ref[rows, :], g0_ref[...], b0_ref[...])
        h = _layer_norm(alpha * h0 + pre_ref[rows, :], g1_ref[...], b1_ref[...])
        hbs.append(h.astype(BF16))
        accs.append(alpha * h)
    for c in range(d_ff // ff_chunk):
        lo = c * ff_chunk
        for i in range(len(blocks)):
            a = jnp.maximum(
                _dot(hbs[i], wu_ref[:, lo:lo + ff_chunk]) + bu_ref[:, lo:lo + ff_chunk], 0.0)
            accs[i] = accs[i] + _dot((a * a).astype(BF16), wd_ref[lo:lo + ff_chunk, :])
    for rows, acc in zip(blocks, accs):
        out_ref[rows, :] = _layer_norm(acc, g2_ref[...], b2_ref[...])


def _ffn(x, pre, g0, b0, g1, b1, wu, bu, wd, g2, b2, *, alpha):
    rows, d_model = x.shape
    tile = TILE_FFN
    assert rows % tile == 0
    d_ff = wu.shape[1]
    return pl.pallas_call(
        functools.partial(_ffn_kernel, alpha=alpha, ff_chunk=min(2048, d_ff),
                          row_block=min(ROW_BLOCK, tile)),
        grid=(rows // tile,),
        in_specs=[pl.BlockSpec((tile, d_model), lambda i: (i, 0)),
                  pl.BlockSpec((tile, d_model), lambda i: (i, 0)),
                  _const_spec((1, d_model)), _const_spec((1, d_model)),
                  _const_spec((1, d_model)), _const_spec((1, d_model)),
                  _const_spec((d_model, d_ff)), _const_spec((1, d_ff)),
                  _const_spec((d_ff, d_model)), _const_spec((1, d_model)),
                  _const_spec((1, d_model))],
        out_specs=pl.BlockSpec((tile, d_model), lambda i: (i, 0)),
        out_shape=jax.ShapeDtypeStruct((rows, d_model), F32),
        compiler_params=pltpu.CompilerParams(
            dimension_semantics=("arbitrary",), vmem_limit_bytes=V7X_VMEM_LIMIT),
        name="ffn",
    )(x, pre, g0, b0, g1, b1, wu, bu, wd, g2, b2)


def kernel(x, meta_tokens, ln0_g, ln0_b, w_in, b_in, qk_conv_w, qk_conv_b, s5_lambda_re, s5_lambda_im, s5_log_dt, s5_b_re, s5_b_im, s5_c_re, s5_c_im, s5_d, s5_w_glu, m_norm_g, m_w_out, w_o, ln1_g, ln1_b, w_up, b_up, w_down, ln2_g, ln2_b):
    bsz, seq, d_model = x.shape
    depth = w_in.shape[0]
    assert depth == 1, "the meta-token prologue is written for a single layer"
    assert meta_tokens.shape == (N_META, d_model)
    assert seq % TILE_MIX == 0 and seq % S5_CHUNK == 0 and MLSTM_CHUNK >= S5_CHUNK >= N_META
    alpha = (2.0 * depth) ** 0.25
    d_s5 = d_model // 2
    d_qk = qk_conv_w.shape[-1] // 2
    d_v = m_norm_g.shape[-1]
    dk, dv = d_qk // M_HEADS, d_v // M_HEADS
    gate_lo = d_s5 + 2 * d_qk + 2 * d_v
    gate_hi = gate_lo + 2 * M_HEADS
    assert w_in.shape[-1] == gate_hi + 2 * d_model

    row = lambda a: a.reshape(1, -1).astype(F32)
    w = w_in[0]
    bias = b_in[0]
    qk_lo, qk_hi = d_s5, d_s5 + 2 * d_qk
    w_t = jnp.concatenate([w[:, :qk_lo], w[:, qk_hi:gate_lo], w[:, gate_hi:]], axis=1).T.astype(BF16)
    b_t = jnp.concatenate([bias[:qk_lo], bias[qk_hi:gate_lo], bias[gate_hi:]]).reshape(-1, 1).astype(F32)
    w_qk = w[:, qk_lo:qk_hi].astype(BF16)
    b_qk = row(bias[qk_lo:qk_hi])
    w_gate = w[:, gate_lo:gate_hi]
    wgc = jnp.pad(w_gate, ((0, 0), (0, LANES - 2 * M_HEADS))).astype(BF16)
    bgc = jnp.pad(bias[gate_lo:gate_hi], (0, LANES - 2 * M_HEADS)).reshape(1, LANES).astype(F32)
    wgr = w_gate.T.astype(BF16)
    bgr = bias[gate_lo:gate_hi].reshape(2 * M_HEADS, 1).astype(F32)
    g0, b0 = row(ln0_g), row(ln0_b)
    conv_w = qk_conv_w[0].astype(F32)
    conv_b = row(qk_conv_b[0])
    mngt = jnp.broadcast_to(m_norm_g[0].astype(F32)[:, None], (d_v, MLSTM_CHUNK))
    mixer_consts = (s5_w_glu[0].T.astype(BF16), mngt, m_w_out[0].T.astype(BF16),
                    w_o[0].astype(BF16))
    inproj_consts = (g0, b0, w_t, b_t, w_qk, b_qk, conv_w, conv_b, wgc, bgc, wgr, bgr)

    pad = MLSTM_CHUNK - N_META
    x_meta = jnp.concatenate([jnp.zeros((pad, d_model), x.dtype), meta_tokens.astype(x.dtype)])[None]
    ut_m, q_m, k_m, v_m, o_m, gs_m, gm_m, gcol_m, grow_m, carry_m = _inproj(
        x_meta, *inproj_consts, jnp.zeros((CARRY_ROWS, 2 * d_qk), F32), n_pad=pad)
    zero_state = (jnp.zeros((M_HEADS, dv, dk), F32), jnp.zeros((M_HEADS, 8, dk), F32),
                  jnp.zeros((8, dk), F32))
    _, c_m, n_m, m_m = _mixer(
        jnp.zeros((1, d_s5, MLSTM_CHUNK), BF16), q_m, k_m, v_m, o_m, gs_m, gm_m, gcol_m, grow_m,
        *mixer_consts, *zero_state, n_pad=pad, emit_state=True)

    ut, q, k, v, o, gs, gm, gcol, grow, _ = _inproj(x, *inproj_consts, carry_m[0], n_pad=0)
    s5_consts = _s5_weights(s5_lambda_re[0], s5_lambda_im[0], s5_log_dt[0], s5_b_re[0], s5_b_im[0],
                            s5_c_re[0], s5_c_im[0], s5_d[0])
    yst = _s5(ut, ut_m[0, :, MLSTM_CHUNK - S5_CHUNK:], *s5_consts)
    (pre,) = _mixer(yst, q, k, v, o, gs, gm, gcol, grow, *mixer_consts, c_m[0], n_m[0], m_m[0],
                    n_pad=0, emit_state=False)
    out = _ffn(x.reshape(bsz * seq, d_model), pre.reshape(bsz * seq, d_model), g0, b0,
               row(ln1_g[0]), row(ln1_b[0]), w_up[0].astype(BF16), row(b_up[0]),
               w_down[0].astype(BF16), row(ln2_g[0]), row(ln2_b[0]), alpha=alpha)
    return out.reshape(bsz, seq, d_model)
```
